```python
import math
import jax, jax.numpy as jnp
from jax import lax
import numpy as np

D_MODEL = 4096
BATCH = 4
SEQ = 2048
DEPTH = 1

HEAD_DIM = 128
MIX_WIDTH = D_MODEL
SWA_HEADS = (MIX_WIDTH // 2) // HEAD_DIM
SWA_KV_HEADS = SWA_HEADS // 4
SWA_GROUP = SWA_HEADS // SWA_KV_HEADS
WINDOW = 128
BLOCK = 128
DIFF_HEADS = (MIX_WIDTH // 2) // HEAD_DIM
DIFF_QK_DIM = HEAD_DIM // 2
DIFF_V_DIM = HEAD_DIM
D_FF = 4 * D_MODEL
PLE_DIM = 256
ROPE_THETA = 10000.0
NORM_EPS = 1e-6
NEG_INF = -1e30

SWA_Q_COLS = SWA_HEADS * HEAD_DIM
SWA_KV_COLS = SWA_KV_HEADS * HEAD_DIM
DIFF_QK_COLS = 2 * DIFF_HEADS * DIFF_QK_DIM
DIFF_V_COLS = DIFF_HEADS * DIFF_V_DIM
IN_COLS = SWA_Q_COLS + 2 * SWA_KV_COLS + 2 * DIFF_QK_COLS + DIFF_V_COLS

kernel_name = "hymba_swa_sink_diffattn_sqrelu_ple"


def rmsnorm(x, g):
    xf = x.astype(jnp.float32)
    y = xf * lax.rsqrt(jnp.mean(xf * xf, axis=-1, keepdims=True) + NORM_EPS)
    return (y * g.astype(jnp.float32)).astype(x.dtype)


def rope_tables(seq_len, dim):
    pos = jnp.arange(seq_len, dtype=jnp.float32)
    inv = 1.0 / (ROPE_THETA ** (jnp.arange(0, dim, 2, dtype=jnp.float32) / dim))
    ang = pos[:, None] * inv[None, :]
    ang = jnp.concatenate([ang, ang], axis=-1)
    return jnp.cos(ang), jnp.sin(ang)


def apply_rope(x, cos, sin):
    shape = (1, cos.shape[0]) + (1,) * (x.ndim - 3) + (cos.shape[1],)
    c, s = cos.reshape(shape), sin.reshape(shape)
    xf = x.astype(jnp.float32)
    x1, x2 = jnp.split(xf, 2, axis=-1)
    rot = jnp.concatenate([-x2, x1], axis=-1)
    return (xf * c + rot * s).astype(x.dtype)


def sliding_window_sink_attention(q, k, v, sinks):
    B, S = q.shape[0], q.shape[1]
    nb = S // BLOCK
    qb = q.reshape(B, nb, BLOCK, SWA_KV_HEADS, SWA_GROUP, HEAD_DIM)
    kb = k.reshape(B, nb, BLOCK, SWA_KV_HEADS, HEAD_DIM)
    vb = v.reshape(B, nb, BLOCK, SWA_KV_HEADS, HEAD_DIM)
    pad = lambda t: jnp.concatenate([jnp.zeros_like(t[:, :1]), t[:, :-1]], axis=1)
    kk = jnp.concatenate([pad(kb), kb], axis=2)
    vv = jnp.concatenate([pad(vb), vb], axis=2)
    scale = 1.0 / math.sqrt(HEAD_DIM)
    s = jnp.einsum('bnqkgd,bnjkd->bnkgqj', qb, kk).astype(jnp.float32) * scale
    blk = jnp.arange(nb)[:, None, None] * BLOCK
    qpos = blk + jnp.arange(BLOCK)[None, :, None]
    kpos = blk - BLOCK + jnp.arange(2 * BLOCK)[None, None, :]
    valid = (kpos <= qpos) & (qpos - kpos < WINDOW) & (kpos >= 0)
    s = jnp.where(valid[None, :, None, None], s, NEG_INF)
    sink = jnp.broadcast_to(
        sinks.astype(jnp.float32).reshape(1, 1, SWA_KV_HEADS, SWA_GROUP, 1, 1),
        s.shape[:-1] + (1,))
    probs = jax.nn.softmax(jnp.concatenate([s, sink], axis=-1), axis=-1)[..., :-1]
    o = jnp.einsum('bnkgqj,bnjkd->bnqkgd', probs.astype(v.dtype), vv)
    return o.reshape(B, S, SWA_HEADS * HEAD_DIM)


def differential_attention(q, k, v, lam):
    B, S = q.shape[0], q.shape[1]
    nb = S // BLOCK
    qb = q.reshape(B, nb, BLOCK, 2, DIFF_HEADS, DIFF_QK_DIM).transpose(1, 0, 2, 3, 4, 5)
    kpos = jnp.arange(S)
    scale = 1.0 / math.sqrt(DIFF_QK_DIM)

    def one_block(args):
        qi, i = args
        s = jnp.einsum('bqchd,bkchd->bchqk', qi, k).astype(jnp.float32) * scale
        qpos = i * BLOCK + jnp.arange(BLOCK)
        causal = kpos[None, :] <= qpos[:, None]
        s = jnp.where(causal, s, NEG_INF)
        pr = jax.nn.softmax(s, axis=-1)
        w = (pr[:, 0] - lam * pr[:, 1]).astype(v.dtype)
        return jnp.einsum('bhqk,bkhd->bqhd', w, v)

    o = lax.map(one_block, (qb, jnp.arange(nb)))
    return o.transpose(1, 0, 2, 3, 4).reshape(B, S, DIFF_HEADS, DIFF_V_DIM)


def setup_inputs(seed: int = 0) -> dict:
    key = jax.random.key(seed)
    ks = jax.random.split(key, 24)
    nrm = lambda k, shape, sc: jax.random.normal(k, shape, jnp.float32) * sc
    gain = lambda k, n: 1.0 + 0.02 * jax.random.normal(k, (DEPTH, n), jnp.float32)
    return {
        "x": nrm(ks[0], (BATCH, SEQ, D_MODEL), 1.0),
        "p": nrm(ks[1], (DEPTH, BATCH, SEQ, PLE_DIM), 1.0),
        "attn_norm_g": gain(ks[2], D_MODEL),
        "w_in": nrm(ks[3], (DEPTH, D_MODEL, IN_COLS), D_MODEL ** -0.5),
        "swa_q_norm_g": gain(ks[4], HEAD_DIM),
        "swa_k_norm_g": gain(ks[5], HEAD_DIM),
        "swa_sinks": nrm(ks[6], (DEPTH, SWA_HEADS), 0.5),
        "diff_q_norm_g": gain(ks[7], DIFF_QK_DIM),
        "diff_k_norm_g": gain(ks[8], DIFF_QK_DIM),
        "diff_lambda_q1": nrm(ks[9], (DEPTH, DIFF_QK_DIM), 0.1),
        "diff_lambda_k1": nrm(ks[10], (DEPTH, DIFF_QK_DIM), 0.1),
        "diff_lambda_q2": nrm(ks[11], (DEPTH, DIFF_QK_DIM), 0.1),
        "diff_lambda_k2": nrm(ks[12], (DEPTH, DIFF_QK_DIM), 0.1),
        "diff_subln_g": gain(ks[13], DIFF_V_DIM),
        "w_o": nrm(ks[14], (DEPTH, MIX_WIDTH, D_MODEL), MIX_WIDTH ** -0.5),
        "mlp_norm_g": gain(ks[15], D_MODEL),
        "w_up": nrm(ks[16], (DEPTH, D_MODEL, D_FF), D_MODEL ** -0.5),
        "w_down": nrm(ks[17], (DEPTH, D_FF, D_MODEL), D_FF ** -0.5),
        "w_ple_proj": nrm(ks[18], (DEPTH, PLE_DIM, D_MODEL), PLE_DIM ** -0.5),
        "ple_norm_g": gain(ks[19], D_MODEL),
        "w_ple_gate": nrm(ks[20], (DEPTH, D_MODEL, D_MODEL), D_MODEL ** -0.5),
    }


def reference(x, p, attn_norm_g, w_in, swa_q_norm_g, swa_k_norm_g, swa_sinks,
              diff_q_norm_g, diff_k_norm_g, diff_lambda_q1, diff_lambda_k1,
              diff_lambda_q2, diff_lambda_k2, diff_subln_g, w_o, mlp_norm_g,
              w_up, w_down, w_ple_proj, ple_norm_g, w_ple_gate):
    B, S, _ = x.shape
    cos_a, sin_a = rope_tables(S, HEAD_DIM)
    cos_b, sin_b = rope_tables(S, DIFF_QK_DIM)
    splits = np.cumsum([SWA_Q_COLS, SWA_KV_COLS, SWA_KV_COLS,
                        DIFF_QK_COLS, DIFF_QK_COLS])
    h = x
    for i in range(DEPTH):
        a = rmsnorm(h, attn_norm_g[i])
        proj = a @ w_in[i]
        qa, ka, va, qb, kb, vb = jnp.split(proj, splits, axis=-1)

        qa = qa.reshape(B, S, SWA_HEADS, HEAD_DIM)
        ka = ka.reshape(B, S, SWA_KV_HEADS, HEAD_DIM)
        va = va.reshape(B, S, SWA_KV_HEADS, HEAD_DIM)
        qa = apply_rope(rmsnorm(qa, swa_q_norm_g[i]), cos_a, sin_a)
        ka = apply_rope(rmsnorm(ka, swa_k_norm_g[i]), cos_a, sin_a)
        out_a = sliding_window_sink_attention(qa, ka, va, swa_sinks[i])

        qb = qb.reshape(B, S, 2, DIFF_HEADS, DIFF_QK_DIM)
        kb = kb.reshape(B, S, 2, DIFF_HEADS, DIFF_QK_DIM)
        vb = vb.reshape(B, S, DIFF_HEADS, DIFF_V_DIM)
        qb = apply_rope(rmsnorm(qb, diff_q_norm_g[i]), cos_b, sin_b)
        kb = apply_rope(rmsnorm(kb, diff_k_norm_g[i]), cos_b, sin_b)
        lambda_init = 0.8 - 0.6 * math.exp(-0.3 * i)
        lam = (jnp.exp(jnp.sum(diff_lambda_q1[i].astype(jnp.float32) * diff_lambda_k1[i].astype(jnp.float32)))
               - jnp.exp(jnp.sum(diff_lambda_q2[i].astype(jnp.float32) * diff_lambda_k2[i].astype(jnp.float32)))
               + lambda_init)
        ob = differential_attention(qb, kb, vb, lam)
        out_b = (rmsnorm(ob, diff_subln_g[i]) * (1.0 - lambda_init)).reshape(B, S, DIFF_V_COLS)

        h = h + jnp.concatenate([out_a, out_b], axis=-1) @ w_o[i]

        m = rmsnorm(h, mlp_norm_g[i])
        h = h + jnp.square(jax.nn.relu(m @ w_up[i])) @ w_down[i]

        pe = rmsnorm(p[i] @ w_ple_proj[i], ple_norm_g[i])
        h = h + jax.nn.sigmoid(h @ w_ple_gate[i]) * pe
    return h
```

```python
import functools
import math

import jax
import jax.numpy as jnp
from jax import lax
from jax.experimental import pallas as pl
from jax.experimental.pallas import tpu as pltpu

F32 = jnp.float32
BF16 = jnp.bfloat16

D_MODEL = 4096
HEAD_DIM = 128
SWA_HEADS = 16
SWA_KV_HEADS = 4
SWA_GROUP = SWA_HEADS // SWA_KV_HEADS
BLOCK = 128
DIFF_HEADS = 16
DIFF_QK_DIM = 64
D_FF = 4 * D_MODEL
PLE_DIM = 256
ROPE_THETA = 10000.0
NORM_EPS = 1e-6
NEG_INF = -1e30

SWA_Q_COLS = SWA_HEADS * HEAD_DIM
SWA_KV_COLS = SWA_KV_HEADS * HEAD_DIM
DIFF_QK_COLS = 2 * DIFF_HEADS * DIFF_QK_DIM
DIFF_V_COLS = DIFF_HEADS * HEAD_DIM
IN_COLS = SWA_Q_COLS + 2 * SWA_KV_COLS + 2 * DIFF_QK_COLS + DIFF_V_COLS
MIX_HALF = SWA_Q_COLS

LANES = 128
VMEM_LIMIT = 56 * 1024 * 1024

KA_BLK = SWA_Q_COLS // LANES
VA_BLK = KA_BLK + SWA_KV_COLS // LANES
QB_BLK = VA_BLK + SWA_KV_COLS // LANES
KB_BLK = QB_BLK + DIFF_QK_COLS // LANES
VB_BLK = KB_BLK + DIFF_QK_COLS // LANES

IN_TN = 512
IN_TILES = IN_COLS // IN_TN
_A_END = (SWA_Q_COLS + SWA_KV_COLS) // IN_TN
_VA_TILE = _A_END
_B_START = _VA_TILE + 1
_B_END = _B_START + 2 * DIFF_QK_COLS // IN_TN


def _params(sem):
    return pltpu.CompilerParams(dimension_semantics=sem, vmem_limit_bytes=VMEM_LIMIT)


def _rmsnorm_kernel(x_ref, g_ref, o_ref):
    x = x_ref[...]
    ms = jnp.mean(x * x, axis=-1, keepdims=True)
    o_ref[...] = (x * lax.rsqrt(ms + NORM_EPS) * g_ref[...]).astype(o_ref.dtype)


def _rmsnorm_rows(x2d, g, tm=512):
    m, d = x2d.shape
    return pl.pallas_call(
        _rmsnorm_kernel,
        grid=(m // tm,),
        in_specs=[pl.BlockSpec((tm, d), lambda i: (i, 0)),
                  pl.BlockSpec((1, d), lambda i: (0, 0))],
        out_specs=pl.BlockSpec((tm, d), lambda i: (i, 0)),
        out_shape=jax.ShapeDtypeStruct((m, d), BF16),
        compiler_params=_params(("parallel",)),
        name="rmsnorm_rows",
    )(x2d, g.reshape(1, d))


def _in_proj_kernel(a_ref, w_ref, g_ref, ca_ref, sa_ref, cb_ref, sbl_ref, sbh_ref, o_ref):
    n = pl.program_id(0)
    acc = jnp.dot(a_ref[...], w_ref[...], preferred_element_type=F32)
    nsub = IN_TN // LANES

    is_a = n < _A_END
    is_b = (n >= _B_START) & (n < _B_END)
    is_plain = jnp.logical_not(is_a | is_b)

    @pl.when(is_plain)
    def _():
        o_ref[...] = acc.astype(o_ref.dtype)

    @pl.when(is_a)
    def _():
        for c in range(nsub):
            sl = slice(c * LANES, (c + 1) * LANES)
            y = acc[:, sl]
            ms = jnp.mean(y * y, axis=-1, keepdims=True)
            yn = y * lax.rsqrt(ms + NORM_EPS) * g_ref[0][:, sl]
            out = yn * ca_ref[...] + pltpu.roll(yn, HEAD_DIM // 2, 1) * sa_ref[...]
            o_ref[:, sl] = out.astype(o_ref.dtype)

    @pl.when(is_b)
    def _():
        lane = lax.broadcasted_iota(jnp.int32, (1, LANES), 1)
        lo = lane < DIFF_QK_DIM
        for c in range(nsub):
            sl = slice(c * LANES, (c + 1) * LANES)
            y = acc[:, sl]
            sq = y * y
            s_lo = jnp.sum(jnp.where(lo, sq, 0.0), axis=-1, keepdims=True)
            s_hi = jnp.sum(jnp.where(lo, 0.0, sq), axis=-1, keepdims=True)
            ms = jnp.where(lo, s_lo, s_hi) * (1.0 / DIFF_QK_DIM)
            yn = y * lax.rsqrt(ms + NORM_EPS) * g_ref[0][:, sl]
            out = (yn * cb_ref[...]
                   + pltpu.roll(yn, LANES - DIFF_QK_DIM // 2, 1) * sbl_ref[...]
                   + pltpu.roll(yn, DIFF_QK_DIM // 2, 1) * sbh_ref[...])
            o_ref[:, sl] = out.astype(o_ref.dtype)


def _in_proj(a, w_bf, gains, tables, seq, tm=1024):
    m, k = a.shape
    pos_blocks = seq // tm
    tab_spec = pl.BlockSpec((tm, LANES), lambda n, i: (i % pos_blocks, 0))
    return pl.pallas_call(
        _in_proj_kernel,
        grid=(IN_TILES, m // tm),
        in_specs=[pl.BlockSpec((tm, k), lambda n, i: (i, 0)),
                  pl.BlockSpec((k, IN_TN), lambda n, i: (0, n)),
                  pl.BlockSpec((1, 1, IN_TN), lambda n, i: (n, 0, 0)),
                  tab_spec, tab_spec, tab_spec, tab_spec, tab_spec],
        out_specs=pl.BlockSpec((tm, IN_TN), lambda n, i: (i, n)),
        out_shape=jax.ShapeDtypeStruct((m, IN_COLS), BF16),
        compiler_params=_params(("parallel", "parallel")),
        name="in_proj",
    )(a, w_bf, gains, *tables)


def _swa_kernel(sink_ref, q_ref, kp_ref, kc_ref, vp_ref, vc_ref, o_ref):
    kvh = pl.program_id(1)
    i = pl.program_id(2)
    scale = 1.0 / math.sqrt(HEAD_DIM)
    kk = jnp.concatenate([kp_ref[...], kc_ref[...]], axis=0)
    vv = jnp.concatenate([vp_ref[...], vc_ref[...]], axis=0)
    r = lax.broadcasted_iota(jnp.int32, (BLOCK, 2 * BLOCK), 0)
    c = lax.broadcasted_iota(jnp.int32, (BLOCK, 2 * BLOCK), 1)
    valid = (c > r) & (c <= r + BLOCK) & ((i > 0) | (c >= BLOCK))
    for g in range(SWA_GROUP):
        sl = slice(g * HEAD_DIM, (g + 1) * HEAD_DIM)
        s = lax.dot_general(q_ref[:, sl], kk, (((1,), (1,)), ((), ())),
                            preferred_element_type=F32) * scale
        s = jnp.where(valid, s, NEG_INF)
        sink = sink_ref[kvh * SWA_GROUP + g]
        mx = jnp.maximum(jnp.max(s, axis=-1, keepdims=True), sink)
        p = jnp.exp(s - mx)
        denom = jnp.sum(p, axis=-1, keepdims=True) + jnp.exp(sink - mx)
        o = jnp.dot(p.astype(BF16), vv, preferred_element_type=F32)
        o_ref[:, sl] = (o / denom).astype(o_ref.dtype)


def _swa_attention(proj, sinks, batch, seq):
    nb = seq // BLOCK
    gw = SWA_GROUP * HEAD_DIM
    row = lambda b, h, i: b * nb + i
    prev = lambda b, h, i: b * nb + jnp.maximum(i - 1, 0)
    return pl.pallas_call(
        _swa_kernel,
        grid=(batch, SWA_KV_HEADS, nb),
        in_specs=[pl.BlockSpec(memory_space=pltpu.SMEM),
                  pl.BlockSpec((BLOCK, gw), lambda b, h, i: (row(b, h, i), h)),
                  pl.BlockSpec((BLOCK, HEAD_DIM), lambda b, h, i: (prev(b, h, i), KA_BLK + h)),
                  pl.BlockSpec((BLOCK, HEAD_DIM), lambda b, h, i: (row(b, h, i), KA_BLK + h)),
                  pl.BlockSpec((BLOCK, HEAD_DIM), lambda b, h, i: (prev(b, h, i), VA_BLK + h)),
                  pl.BlockSpec((BLOCK, HEAD_DIM), lambda b, h, i: (row(b, h, i), VA_BLK + h))],
        out_specs=pl.BlockSpec((BLOCK, gw), lambda b, h, i: (row(b, h, i), h)),
        out_shape=jax.ShapeDtypeStruct((batch * seq, MIX_HALF), BF16),
        compiler_params=_params(("parallel", "parallel", "parallel")),
        name="swa_attention",
    )(sinks, proj, proj, proj, proj, proj)


DIFF_TQ = 256


def _diff_kernel(lq1_ref, lk1_ref, lq2_ref, lk2_ref, subg_ref,
                 q1_ref, q2_ref, k1_ref, k2_ref, v_ref, o_ref,
                 m_ref, l_ref, acc_ref, *, lambda_init):
    qi = pl.program_id(2)
    tq = DIFF_TQ
    lane = lax.broadcasted_iota(jnp.int32, (tq, LANES), 1)
    qs = []
    for q_ref in (q1_ref, q2_ref):
        q = q_ref[...]
        for e in range(2):
            keep = (lane >= e * DIFF_QK_DIM) & (lane < (e + 1) * DIFF_QK_DIM)
            qs.append(jnp.where(keep, q, jnp.zeros_like(q)))

    m_ref[...] = jnp.full(m_ref.shape, NEG_INF, F32)
    l_ref[...] = jnp.zeros(l_ref.shape, F32)
    acc_ref[...] = jnp.zeros(acc_ref.shape, F32)

    row = lax.broadcasted_iota(jnp.int32, (tq, tq), 0)
    col = lax.broadcasted_iota(jnp.int32, (tq, tq), 1)
    causal = col <= row

    def step(j, masked):
        off = pl.multiple_of(j * tq, tq)
        kb = (k1_ref[pl.ds(off, tq), :], k2_ref[pl.ds(off, tq), :])
        vb = v_ref[pl.ds(off, tq), :]
        for mp in range(2):
            for e in range(2):
                slot = 2 * mp + e
                s = lax.dot_general(qs[slot], kb[mp], (((1,), (1,)), ((), ())),
                                    preferred_element_type=F32)
                if masked:
                    s = jnp.where(causal, s, NEG_INF)
                m_prev = m_ref[slot]
                m_new = jnp.maximum(m_prev, jnp.max(s, axis=-1, keepdims=True))
                alpha = jnp.exp(m_prev - m_new)
                p = jnp.exp(s - m_new)
                l_ref[slot] = alpha * l_ref[slot] + jnp.sum(p, axis=-1, keepdims=True)
                pv = jnp.dot(p.astype(BF16), vb[:, e * HEAD_DIM:(e + 1) * HEAD_DIM],
                             preferred_element_type=F32)
                acc_ref[slot] = alpha * acc_ref[slot] + pv
                m_ref[slot] = m_new

    def body(j, carry):
        step(j, False)
        return carry

    lax.fori_loop(0, qi, body, 0)
    step(qi, True)

    lam = (jnp.exp(jnp.sum(lq1_ref[...] * lk1_ref[...], axis=-1, keepdims=True))
           - jnp.exp(jnp.sum(lq2_ref[...] * lk2_ref[...], axis=-1, keepdims=True))
           + lambda_init)
    for e in range(2):
        o = acc_ref[e] / l_ref[e] - lam * (acc_ref[2 + e] / l_ref[2 + e])
        ms = jnp.mean(o * o, axis=-1, keepdims=True)
        on = o * lax.rsqrt(ms + NORM_EPS) * subg_ref[...]
        o_ref[:, e * HEAD_DIM:(e + 1) * HEAD_DIM] = (on * (1.0 - lambda_init)).astype(o_ref.dtype)


def _diff_attention(proj, lq1, lk1, lq2, lk2, subg, batch, seq, lambda_init):
    tq = DIFF_TQ
    nq = seq // tq
    pairs = DIFF_HEADS // 2
    map_blks = DIFF_HEADS * DIFF_QK_DIM // LANES
    vec = lambda d: pl.BlockSpec((1, d), lambda b, h, i: (0, 0))
    return pl.pallas_call(
        functools.partial(_diff_kernel, lambda_init=lambda_init),
        grid=(batch, pairs, nq),
        in_specs=[vec(DIFF_QK_DIM), vec(DIFF_QK_DIM), vec(DIFF_QK_DIM), vec(DIFF_QK_DIM),
                  vec(HEAD_DIM),
                  pl.BlockSpec((tq, LANES), lambda b, h, i: (b * nq + i, QB_BLK + h)),
                  pl.BlockSpec((tq, LANES), lambda b, h, i: (b * nq + i, QB_BLK + map_blks + h)),
                  pl.BlockSpec((seq, LANES), lambda b, h, i: (b, KB_BLK + h)),
                  pl.BlockSpec((seq, LANES), lambda b, h, i: (b, KB_BLK + map_blks + h)),
                  pl.BlockSpec((seq, 2 * HEAD_DIM), lambda b, h, i: (b, VB_BLK // 2 + h))],
        out_specs=pl.BlockSpec((tq, 2 * HEAD_DIM), lambda b, h, i: (b * nq + i, h)),
        out_shape=jax.ShapeDtypeStruct((batch * seq, MIX_HALF), BF16),
        scratch_shapes=[pltpu.VMEM((4, tq, 1), F32),
                        pltpu.VMEM((4, tq, 1), F32),
                        pltpu.VMEM((4, tq, HEAD_DIM), F32)],
        compiler_params=_params(("parallel", "parallel", "arbitrary")),
        name="diff_attention",
    )(lq1.reshape(1, -1), lk1.reshape(1, -1), lq2.reshape(1, -1), lk2.reshape(1, -1),
      subg.reshape(1, -1), proj, proj, proj, proj, proj)


def _out_proj_kernel(oa_ref, ob_ref, w_ref, x_ref, o_ref):
    acc = jnp.dot(oa_ref[...], w_ref[:MIX_HALF, :], preferred_element_type=F32)
    acc += jnp.dot(ob_ref[...], w_ref[MIX_HALF:, :], preferred_element_type=F32)
    o_ref[...] = x_ref[...] + acc


def _out_proj(oa, ob, w_bf, x2d, tm=512, tn=1024):
    m, d = x2d.shape
    return pl.pallas_call(
        _out_proj_kernel,
        grid=(d // tn, m // tm),
        in_specs=[pl.BlockSpec((tm, MIX_HALF), lambda n, i: (i, 0)),
                  pl.BlockSpec((tm, MIX_HALF), lambda n, i: (i, 0)),
                  pl.BlockSpec((2 * MIX_HALF, tn), lambda n, i: (0, n)),
                  pl.BlockSpec((tm, tn), lambda n, i: (i, n))],
        out_specs=pl.BlockSpec((tm, tn), lambda n, i: (i, n)),
        out_shape=jax.ShapeDtypeStruct((m, d), F32),
        compiler_params=_params(("parallel", "parallel")),
        name="out_proj",
    )(oa, ob, w_bf, x2d)


def _up_kernel(a_ref, w_ref, o_ref):
    acc = jnp.dot(a_ref[...], w_ref[...], preferred_element_type=F32)
    o_ref[...] = jnp.square(jnp.maximum(acc, 0.0)).astype(o_ref.dtype)


def _mlp_up(a, w_bf, tm=1024, tn=1024):
    m, k = a.shape
    n = w_bf.shape[1]
    return pl.pallas_call(
        _up_kernel,
        grid=(n // tn, m // tm),
        in_specs=[pl.BlockSpec((tm, k), lambda j, i: (i, 0)),
                  pl.BlockSpec((k, tn), lambda j, i: (0, j))],
        out_specs=pl.BlockSpec((tm, tn), lambda j, i: (i, j)),
        out_shape=jax.ShapeDtypeStruct((m, n), BF16),
        compiler_params=_params(("parallel", "parallel")),
        name="mlp_up",
    )(a, w_bf)


def _down_kernel(u_ref, w_ref, h_ref, o_ref, ob_ref, acc_ref):
    kk = pl.program_id(2)

    @pl.when(kk == 0)
    def _():
        acc_ref[...] = h_ref[...]

    acc_ref[...] += jnp.dot(u_ref[...], w_ref[...], preferred_element_type=F32)

    @pl.when(kk == pl.num_programs(2) - 1)
    def _():
        h = acc_ref[...]
        o_ref[...] = h
        ob_ref[...] = h.astype(ob_ref.dtype)


def _mlp_down(u, w_bf, h2d, tm=1024, tn=1024, tk=2048):
    m, k = u.shape
    d = w_bf.shape[1]
    return pl.pallas_call(
        _down_kernel,
        grid=(d // tn, m // tm, k // tk),
        in_specs=[pl.BlockSpec((tm, tk), lambda j, i, kk: (i, kk)),
                  pl.BlockSpec((tk, tn), lambda j, i, kk: (kk, j)),
                  pl.BlockSpec((tm, tn), lambda j, i, kk: (i, j))],
        out_specs=[pl.BlockSpec((tm, tn), lambda j, i, kk: (i, j)),
                   pl.BlockSpec((tm, tn), lambda j, i, kk: (i, j))],
        out_shape=[jax.ShapeDtypeStruct((m, d), F32),
                   jax.ShapeDtypeStruct((m, d), BF16)],
        scratch_shapes=[pltpu.VMEM((tm, tn), F32)],
        compiler_params=_params(("parallel", "parallel", "arbitrary")),
        name="mlp_down",
    )(u, w_bf, h2d)


def _ple_kernel(p_ref, w_ref, g_ref, o_ref):
    y = jnp.dot(p_ref[...].astype(BF16), w_ref[...], preferred_element_type=F32)
    ms = jnp.mean(y * y, axis=-1, keepdims=True)
    o_ref[...] = y * lax.rsqrt(ms + NORM_EPS) * g_ref[...]


def _ple_embed(p2d, w_bf, g, tm=512):
    m, k = p2d.shape
    d = w_bf.shape[1]
    return pl.pallas_call(
        _ple_kernel,
        grid=(m // tm,),
        in_specs=[pl.BlockSpec((tm, k), lambda i: (i, 0)),
                  pl.BlockSpec((k, d), lambda i: (0, 0)),
                  pl.BlockSpec((1, d), lambda i: (0, 0))],
        out_specs=pl.BlockSpec((tm, d), lambda i: (i, 0)),
        out_shape=jax.ShapeDtypeStruct((m, d), F32),
        compiler_params=_params(("parallel",)),
        name="ple_embed",
    )(p2d, w_bf, g.reshape(1, d))


def _gate_kernel(hb_ref, w_ref, h_ref, pe_ref, o_ref):
    z = jnp.dot(hb_ref[...], w_ref[...], preferred_element_type=F32)
    o_ref[...] = h_ref[...] + jax.nn.sigmoid(z) * pe_ref[...]


def _ple_gate(hb, w_bf, h2d, pe, tm=512, tn=1024):
    m, d = h2d.shape
    tile = pl.BlockSpec((tm, tn), lambda j, i: (i, j))
    return pl.pallas_call(
        _gate_kernel,
        grid=(d // tn, m // tm),
        in_specs=[pl.BlockSpec((tm, d), lambda j, i: (i, 0)),
                  pl.BlockSpec((d, tn), lambda j, i: (0, j)),
                  tile, tile],
        out_specs=tile,
        out_shape=jax.ShapeDtypeStruct((m, d), F32),
        compiler_params=_params(("parallel", "parallel")),
        name="ple_gate",
    )(hb, w_bf, h2d, pe)


def _rope_tables(seq):
    pos = jnp.arange(seq, dtype=F32)

    def tab(dim):
        inv = 1.0 / (ROPE_THETA ** (jnp.arange(0, dim, 2, dtype=F32) / dim))
        ang = pos[:, None] * inv[None, :]
        ang = jnp.concatenate([ang, ang], axis=-1)
        return jnp.cos(ang), jnp.sin(ang)

    lane = jnp.arange(LANES)
    cos_a, sin_a = tab(HEAD_DIM)
    sin_a = jnp.where(lane < HEAD_DIM // 2, -sin_a, sin_a)
    cos_b, sin_b = tab(DIFF_QK_DIM)
    cos_b = jnp.concatenate([cos_b, cos_b], axis=-1)
    sin_b = jnp.concatenate([sin_b, sin_b], axis=-1)
    first_half = (lane % DIFF_QK_DIM) < DIFF_QK_DIM // 2
    sin_b_lo = jnp.where(first_half, -sin_b, 0.0)
    sin_b_hi = jnp.where(first_half, 0.0, sin_b)
    return cos_a, sin_a, cos_b, sin_b_lo, sin_b_hi


def _gain_rows(swa_q_g, swa_k_g, diff_q_g, diff_k_g):
    ones = jnp.ones((IN_TN,), F32)
    qa = jnp.tile(swa_q_g, IN_TN // HEAD_DIM)
    ka = jnp.tile(swa_k_g, IN_TN // HEAD_DIM)
    qb = jnp.tile(diff_q_g, IN_TN // DIFF_QK_DIM) * (1.0 / math.sqrt(DIFF_QK_DIM))
    kb = jnp.tile(diff_k_g, IN_TN // DIFF_QK_DIM)
    rows = ([qa] * (SWA_Q_COLS // IN_TN) + [ka] * (SWA_KV_COLS // IN_TN)
            + [ones] * (SWA_KV_COLS // IN_TN)
            + [qb] * (DIFF_QK_COLS // IN_TN) + [kb] * (DIFF_QK_COLS // IN_TN)
            + [ones] * (DIFF_V_COLS // IN_TN))
    return jnp.stack(rows).reshape(IN_TILES, 1, IN_TN)


def kernel(x, p, attn_norm_g, w_in, swa_q_norm_g, swa_k_norm_g, swa_sinks, diff_q_norm_g, diff_k_norm_g, diff_lambda_q1, diff_lambda_k1, diff_lambda_q2, diff_lambda_k2, diff_subln_g, w_o, mlp_norm_g, w_up, w_down, w_ple_proj, ple_norm_g, w_ple_gate):
    batch, seq, d = x.shape
    depth = w_in.shape[0]
    tables = _rope_tables(seq)
    h = x.reshape(batch * seq, d)
    for i in range(depth):
        lambda_init = 0.8 - 0.6 * math.exp(-0.3 * i)
        a = _rmsnorm_rows(h, attn_norm_g[i])
        gains = _gain_rows(swa_q_norm_g[i], swa_k_norm_g[i], diff_q_norm_g[i], diff_k_norm_g[i])
        proj = _in_proj(a, w_in[i].astype(BF16), gains, tables, seq)
        out_a = _swa_attention(proj, swa_sinks[i], batch, seq)
        out_b = _diff_attention(proj, diff_lambda_q1[i], diff_lambda_k1[i], diff_lambda_q2[i],
                                diff_lambda_k2[i], diff_subln_g[i], batch, seq, lambda_init)
        h = _out_proj(out_a, out_b, w_o[i].astype(BF16), h)
        m = _rmsnorm_rows(h, mlp_norm_g[i])
        u = _mlp_up(m, w_up[i].astype(BF16))
        h, hb = _mlp_down(u, w_down[i].astype(BF16), h)
        pe = _ple_embed(p[i].reshape(batch * seq, PLE_DIM), w_ple_proj[i].astype(BF16), ple_norm_g[i])
        h = _ple_gate(hb, w_ple_gate[i].astype(BF16), h, pe)
    return h.reshape(batch, seq, d)
```

```python
import functools
import math

import jax
import jax.numpy as jnp
from jax import lax
from jax.experimental import pallas as pl
from jax.experimental.pallas import tpu as pltpu

F32 = jnp.float32
BF16 = jnp.bfloat16

D_MODEL = 4096
HEAD_DIM = 128
SWA_HEADS = 16
SWA_KV_HEADS = 4
SWA_GROUP = SWA_HEADS // SWA_KV_HEADS
BLOCK = 128
DIFF_HEADS = 16
DIFF_QK_DIM = 64
D_FF = 4 * D_MODEL
PLE_DIM = 256
ROPE_THETA = 10000.0
NORM_EPS = 1e-6
NEG_INF = -1e30
LOG2E = math.log2(math.e)
SCORE_BOUND = 40.0

SWA_Q_COLS = SWA_HEADS * HEAD_DIM
SWA_KV_COLS = SWA_KV_HEADS * HEAD_DIM
DIFF_QK_COLS = 2 * DIFF_HEADS * DIFF_QK_DIM
DIFF_V_COLS = DIFF_HEADS * HEAD_DIM
IN_COLS = SWA_Q_COLS + 2 * SWA_KV_COLS + 2 * DIFF_QK_COLS + DIFF_V_COLS
MIX_HALF = SWA_Q_COLS

LANES = 128
VMEM_LIMIT = 56 * 1024 * 1024

KA_BLK = SWA_Q_COLS // LANES
VA_BLK = KA_BLK + SWA_KV_COLS // LANES
QB_BLK = VA_BLK + SWA_KV_COLS // LANES
KB_BLK = QB_BLK + DIFF_QK_COLS // LANES
VB_BLK = KB_BLK + DIFF_QK_COLS // LANES

IN_TN = 512
IN_TILES = IN_COLS // IN_TN
IN_SUB_ROWS = 256
_A_END = (SWA_Q_COLS + SWA_KV_COLS) // IN_TN
_VA_TILE = _A_END
_B_START = _VA_TILE + 1
_B_END = _B_START + 2 * DIFF_QK_COLS // IN_TN


def _params(sem):
    return pltpu.CompilerParams(dimension_semantics=sem, vmem_limit_bytes=VMEM_LIMIT)


def _rmsnorm_kernel(x_ref, g_ref, o_ref):
    x = x_ref[...]
    ms = jnp.mean(x * x, axis=-1, keepdims=True)
    o_ref[...] = (x * lax.rsqrt(ms + NORM_EPS) * g_ref[...]).astype(o_ref.dtype)


def _rmsnorm_rows(x2d, g, tm=512):
    m, d = x2d.shape
    return pl.pallas_call(
        _rmsnorm_kernel,
        grid=(m // tm,),
        in_specs=[pl.BlockSpec((tm, d), lambda i: (i, 0)),
                  pl.BlockSpec((1, d), lambda i: (0, 0))],
        out_specs=pl.BlockSpec((tm, d), lambda i: (i, 0)),
        out_shape=jax.ShapeDtypeStruct((m, d), BF16),
        compiler_params=_params(("parallel",)),
        name="rmsnorm_rows",
    )(x2d, g.reshape(1, d))


def _in_proj_kernel(a_ref, w_ref, g_ref, ca_ref, sa_ref, cb_ref, sbl_ref, sbh_ref, o_ref):
    n = pl.program_id(0)
    nsub = IN_TN // LANES
    tm = a_ref.shape[0]

    def run(epilogue):
        for r in range(tm // IN_SUB_ROWS):
            rs = slice(r * IN_SUB_ROWS, (r + 1) * IN_SUB_ROWS)
            acc = jnp.dot(a_ref[rs, :], w_ref[...], preferred_element_type=F32)
            epilogue(acc, rs)

    def plain(acc, rs):
        o_ref[rs, :] = acc.astype(o_ref.dtype)

    def heads128(acc, rs):
        for c in range(nsub):
            sl = slice(c * LANES, (c + 1) * LANES)
            y = acc[:, sl]
            ms = jnp.mean(y * y, axis=-1, keepdims=True)
            yn = y * lax.rsqrt(ms + NORM_EPS) * g_ref[0][:, sl]
            out = yn * ca_ref[rs, :] + pltpu.roll(yn, HEAD_DIM // 2, 1) * sa_ref[rs, :]
            o_ref[rs, sl] = out.astype(o_ref.dtype)

    def heads64(acc, rs):
        lane = lax.broadcasted_iota(jnp.int32, (1, LANES), 1)
        lo = lane < DIFF_QK_DIM
        for c in range(nsub):
            sl = slice(c * LANES, (c + 1) * LANES)
            y = acc[:, sl]
            sq = y * y
            s_lo = jnp.sum(jnp.where(lo, sq, 0.0), axis=-1, keepdims=True)
            s_hi = jnp.sum(jnp.where(lo, 0.0, sq), axis=-1, keepdims=True)
            ms = jnp.where(lo, s_lo, s_hi) * (1.0 / DIFF_QK_DIM)
            yn = y * lax.rsqrt(ms + NORM_EPS) * g_ref[0][:, sl]
            out = (yn * cb_ref[rs, :]
                   + pltpu.roll(yn, LANES - DIFF_QK_DIM // 2, 1) * sbl_ref[rs, :]
                   + pltpu.roll(yn, DIFF_QK_DIM // 2, 1) * sbh_ref[rs, :])
            o_ref[rs, sl] = out.astype(o_ref.dtype)

    is_a = n < _A_END
    is_b = (n >= _B_START) & (n < _B_END)
    pl.when(is_a)(lambda: run(heads128))
    pl.when(is_b)(lambda: run(heads64))
    pl.when(jnp.logical_not(is_a | is_b))(lambda: run(plain))


def _in_proj(a, w_bf, gains, tables, seq, tm=1024):
    m, k = a.shape
    pos_blocks = seq // tm
    tab_spec = pl.BlockSpec((tm, LANES), lambda n, i: (i % pos_blocks, 0))
    return pl.pallas_call(
        _in_proj_kernel,
        grid=(IN_TILES, m // tm),
        in_specs=[pl.BlockSpec((tm, k), lambda n, i: (i, 0)),
                  pl.BlockSpec((k, IN_TN), lambda n, i: (0, n)),
                  pl.BlockSpec((1, 1, IN_TN), lambda n, i: (n, 0, 0)),
                  tab_spec, tab_spec, tab_spec, tab_spec, tab_spec],
        out_specs=pl.BlockSpec((tm, IN_TN), lambda n, i: (i, n)),
        out_shape=jax.ShapeDtypeStruct((m, IN_COLS), BF16),
        compiler_params=_params(("parallel", "parallel")),
        name="in_proj",
    )(a, w_bf, gains, *tables)


def _swa_kernel(bounded_ref, sink_ref, sinkrow_ref, q_ref, kp_ref, kc_ref, vp_ref, vc_ref, o_ref):
    i = pl.program_id(1)
    nt = (((1,), (1,)), ((), ()))
    rows = SWA_GROUP * BLOCK
    r = lax.broadcasted_iota(jnp.int32, (rows, 2 * BLOCK), 0) & (BLOCK - 1)
    c = lax.broadcasted_iota(jnp.int32, (rows, 2 * BLOCK), 1)
    valid = (c > r) & (c <= r + BLOCK) & ((i > 0) | (c >= BLOCK))
    ones = jnp.ones((2 * BLOCK, HEAD_DIM), BF16)

    def masked_scores(h):
        ks = slice(h * HEAD_DIM, (h + 1) * HEAD_DIM)
        kk = jnp.concatenate([kp_ref[:, ks], kc_ref[:, ks]], axis=0)
        vv = jnp.concatenate([vp_ref[:, ks], vc_ref[:, ks]], axis=0)
        heads = [h * SWA_GROUP + g for g in range(SWA_GROUP)]
        q4 = jnp.concatenate([q_ref[:, hd * HEAD_DIM:(hd + 1) * HEAD_DIM] for hd in heads], axis=0)
        s = lax.dot_general(q4, kk, nt, preferred_element_type=F32)
        return jnp.where(valid, s, NEG_INF), vv, heads

    @pl.when(bounded_ref[0] != 0)
    def _():
        for h in range(SWA_KV_HEADS):
            s, vv, heads = masked_scores(h)
            vaug = jnp.concatenate([vv, ones], axis=1)
            o = jnp.dot(jnp.exp2(s).astype(BF16), vaug, preferred_element_type=F32)
            for g, hd in enumerate(heads):
                rs = slice(g * BLOCK, (g + 1) * BLOCK)
                sink = jnp.exp2(sinkrow_ref[hd:hd + 1, :] * LOG2E)
                out = o[rs, :HEAD_DIM] / (o[rs, HEAD_DIM:] + sink)
                o_ref[:, hd * HEAD_DIM:(hd + 1) * HEAD_DIM] = out.astype(o_ref.dtype)

    @pl.when(bounded_ref[0] == 0)
    def _():
        for h in range(SWA_KV_HEADS):
            s, vv, heads = masked_scores(h)
            for g, hd in enumerate(heads):
                sg = s[g * BLOCK:(g + 1) * BLOCK]
                sink = sink_ref[hd] * LOG2E
                mx = jnp.maximum(jnp.max(sg, axis=-1, keepdims=True), sink)
                p = jnp.exp2(sg - mx)
                denom = jnp.sum(p, axis=-1, keepdims=True) + jnp.exp2(sink - mx)
                o = jnp.dot(p.astype(BF16), vv, preferred_element_type=F32)
                o_ref[:, hd * HEAD_DIM:(hd + 1) * HEAD_DIM] = (o / denom).astype(o_ref.dtype)


def _swa_attention(proj, bounded, sinks, batch, seq):
    nb = seq // BLOCK
    qw = SWA_HEADS * HEAD_DIM
    kw = SWA_KV_HEADS * HEAD_DIM
    row = lambda b, i: b * nb + i
    prev = lambda b, i: b * nb + jnp.maximum(i - 1, 0)
    k_blk = SWA_Q_COLS // kw
    v_blk = k_blk + 1
    sink_rows = jnp.broadcast_to(sinks[:, None], (SWA_HEADS, HEAD_DIM))
    return pl.pallas_call(
        _swa_kernel,
        grid=(batch, nb),
        in_specs=[pl.BlockSpec(memory_space=pltpu.SMEM),
                  pl.BlockSpec(memory_space=pltpu.SMEM),
                  pl.BlockSpec((SWA_HEADS, HEAD_DIM), lambda b, i: (0, 0)),
                  pl.BlockSpec((BLOCK, qw), lambda b, i: (row(b, i), 0)),
                  pl.BlockSpec((BLOCK, kw), lambda b, i: (prev(b, i), k_blk)),
                  pl.BlockSpec((BLOCK, kw), lambda b, i: (row(b, i), k_blk)),
                  pl.BlockSpec((BLOCK, kw), lambda b, i: (prev(b, i), v_blk)),
                  pl.BlockSpec((BLOCK, kw), lambda b, i: (row(b, i), v_blk))],
        out_specs=pl.BlockSpec((BLOCK, qw), lambda b, i: (row(b, i), 0)),
        out_shape=jax.ShapeDtypeStruct((batch * seq, MIX_HALF), BF16),
        compiler_params=_params(("parallel", "parallel")),
        name="swa_attention",
    )(bounded, sinks, sink_rows, proj, proj, proj, proj, proj)


DIFF_TQ = 256


def _diff_kernel(bounded_ref, lq1_ref, lk1_ref, lq2_ref, lk2_ref, subg_ref,
                 q1_ref, q2_ref, k1_ref, k2_ref, v_ref, o_ref,
                 vaug_ref, acc_ref, m_ref, l_ref, accs_ref, *, lambda_init):
    qi = pl.program_id(2)
    tq = DIFF_TQ
    nt = (((1,), (1,)), ((), ()))
    lane = lax.broadcasted_iota(jnp.int32, (tq, LANES), 1)
    qs = []
    for q_ref in (q1_ref, q2_ref):
        q = q_ref[...]
        for e in range(2):
            keep = (lane >= e * DIFF_QK_DIM) & (lane < (e + 1) * DIFF_QK_DIM)
            qs.append(jnp.where(keep, q, jnp.zeros_like(q)))

    row = lax.broadcasted_iota(jnp.int32, (tq, tq), 0)
    col = lax.broadcasted_iota(jnp.int32, (tq, tq), 1)
    causal = col <= row

    lam = (jnp.exp(jnp.sum(lq1_ref[...] * lk1_ref[...], axis=-1, keepdims=True))
           - jnp.exp(jnp.sum(lq2_ref[...] * lk2_ref[...], axis=-1, keepdims=True))
           + lambda_init)

    def finalize(e, o1, o2):
        o = o1 - lam * o2
        ms = jnp.mean(o * o, axis=-1, keepdims=True)
        on = o * lax.rsqrt(ms + NORM_EPS) * subg_ref[...]
        o_ref[:, e * HEAD_DIM:(e + 1) * HEAD_DIM] = (on * (1.0 - lambda_init)).astype(o_ref.dtype)

    @pl.when(qi == 0)
    def _():
        for e in range(2):
            vaug_ref[e, :, :HEAD_DIM] = v_ref[:, e * HEAD_DIM:(e + 1) * HEAD_DIM]
            vaug_ref[e, :, HEAD_DIM:] = jnp.ones((v_ref.shape[0], HEAD_DIM), BF16)

    @pl.when(bounded_ref[0] != 0)
    def _():
        def step(j, diagonal):
            off = pl.multiple_of(j * tq, tq)
            kb = (k1_ref[pl.ds(off, tq), :], k2_ref[pl.ds(off, tq), :])
            ps = []
            for slot in range(4):
                s = lax.dot_general(qs[slot], kb[slot // 2], nt, preferred_element_type=F32)
                if diagonal:
                    s = jnp.where(causal, s, NEG_INF)
                ps.append(jnp.exp2(s).astype(BF16))
            for e in range(2):
                pp = jnp.concatenate([ps[e], ps[2 + e]], axis=0)
                pv = jnp.dot(pp, vaug_ref[e, pl.ds(off, tq), :], preferred_element_type=F32)
                if diagonal:
                    acc_ref[e] = pv
                else:
                    acc_ref[e] += pv

        step(qi, True)

        def body(jj, carry):
            step(2 * jj, False)
            step(2 * jj + 1, False)
            return carry

        lax.fori_loop(0, qi // 2, body, 0)
        pl.when(qi % 2 == 1)(lambda: step(qi - 1, False))
        for e in range(2):
            a = acc_ref[e]
            finalize(e, a[:tq, :HEAD_DIM] / a[:tq, HEAD_DIM:],
                     a[tq:, :HEAD_DIM] / a[tq:, HEAD_DIM:])

    @pl.when(bounded_ref[0] == 0)
    def _():
        m_ref[...] = jnp.full(m_ref.shape, NEG_INF, F32)
        l_ref[...] = jnp.zeros(l_ref.shape, F32)
        accs_ref[...] = jnp.zeros(accs_ref.shape, F32)

        def step(j, diagonal):
            off = pl.multiple_of(j * tq, tq)
            kb = (k1_ref[pl.ds(off, tq), :], k2_ref[pl.ds(off, tq), :])
            vb = v_ref[pl.ds(off, tq), :]
            for slot in range(4):
                e = slot % 2
                s = lax.dot_general(qs[slot], kb[slot // 2], nt, preferred_element_type=F32)
                if diagonal:
                    s = jnp.where(causal, s, NEG_INF)
                m_prev = m_ref[slot]
                m_new = jnp.maximum(m_prev, jnp.max(s, axis=-1, keepdims=True))
                alpha = jnp.exp2(m_prev - m_new)
                p = jnp.exp2(s - m_new)
                l_ref[slot] = alpha * l_ref[slot] + jnp.sum(p, axis=-1, keepdims=True)
                pv = jnp.dot(p.astype(BF16), vb[:, e * HEAD_DIM:(e + 1) * HEAD_DIM],
                             preferred_element_type=F32)
                accs_ref[slot] = alpha * accs_ref[slot] + pv
                m_ref[slot] = m_new

        def body(j, carry):
            step(j, False)
            return carry

        lax.fori_loop(0, qi, body, 0)
        step(qi, True)
        for e in range(2):
            finalize(e, accs_ref[e] / l_ref[e], accs_ref[2 + e] / l_ref[2 + e])


def _diff_attention(proj, bounded, lq1, lk1, lq2, lk2, subg, batch, seq, lambda_init):
    tq = DIFF_TQ
    nq = seq // tq
    pairs = DIFF_HEADS // 2
    map_blks = DIFF_HEADS * DIFF_QK_DIM // LANES
    vec = lambda d: pl.BlockSpec((1, d), lambda b, h, i: (0, 0))
    return pl.pallas_call(
        functools.partial(_diff_kernel, lambda_init=lambda_init),
        grid=(batch, pairs, nq),
        in_specs=[pl.BlockSpec(memory_space=pltpu.SMEM),
                  vec(DIFF_QK_DIM), vec(DIFF_QK_DIM), vec(DIFF_QK_DIM), vec(DIFF_QK_DIM),
                  vec(HEAD_DIM),
                  pl.BlockSpec((tq, LANES), lambda b, h, i: (b * nq + i, QB_BLK + h)),
                  pl.BlockSpec((tq, LANES), lambda b, h, i: (b * nq + i, QB_BLK + map_blks + h)),
                  pl.BlockSpec((seq, LANES), lambda b, h, i: (b, KB_BLK + h)),
                  pl.BlockSpec((seq, LANES), lambda b, h, i: (b, KB_BLK + map_blks + h)),
                  pl.BlockSpec((seq, 2 * HEAD_DIM), lambda b, h, i: (b, VB_BLK // 2 + h))],
        out_specs=pl.BlockSpec((tq, 2 * HEAD_DIM), lambda b, h, i: (b * nq + i, h)),
        out_shape=jax.ShapeDtypeStruct((batch * seq, MIX_HALF), BF16),
        scratch_shapes=[pltpu.VMEM((2, seq, 2 * HEAD_DIM), BF16),
                        pltpu.VMEM((2, 2 * tq, 2 * HEAD_DIM), F32),
                        pltpu.VMEM((4, tq, 1), F32),
                        pltpu.VMEM((4, tq, 1), F32),
                        pltpu.VMEM((4, tq, HEAD_DIM), F32)],
        compiler_params=_params(("parallel", "parallel", "arbitrary")),
        name="diff_attention",
    )(bounded, lq1.reshape(1, -1), lk1.reshape(1, -1), lq2.reshape(1, -1), lk2.reshape(1, -1),
      subg.reshape(1, -1), proj, proj, proj, proj, proj)


def _out_proj_kernel(oa_ref, ob_ref, w_ref, x_ref, o_ref):
    acc = jnp.dot(oa_ref[...], w_ref[:MIX_HALF, :], preferred_element_type=F32)
    acc += jnp.dot(ob_ref[...], w_ref[MIX_HALF:, :], preferred_element_type=F32)
    o_ref[...] = x_ref[...] + acc


def _out_proj(oa, ob, w_bf, x2d, tm=512, tn=1024):
    m, d = x2d.shape
    return pl.pallas_call(
        _out_proj_kernel,
        grid=(d // tn, m // tm),
        in_specs=[pl.BlockSpec((tm, MIX_HALF), lambda n, i: (i, 0)),
                  pl.BlockSpec((tm, MIX_HALF), lambda n, i: (i, 0)),
                  pl.BlockSpec((2 * MIX_HALF, tn), lambda n, i: (0, n)),
                  pl.BlockSpec((tm, tn), lambda n, i: (i, n))],
        out_specs=pl.BlockSpec((tm, tn), lambda n, i: (i, n)),
        out_shape=jax.ShapeDtypeStruct((m, d), F32),
        compiler_params=_params(("parallel", "parallel")),
        name="out_proj",
    )(oa, ob, w_bf, x2d)


def _up_kernel(a_ref, w_ref, o_ref):
    acc = jnp.dot(a_ref[...], w_ref[...], preferred_element_type=F32)
    o_ref[...] = jnp.square(jnp.maximum(acc, 0.0)).astype(o_ref.dtype)


def _mlp_up(a, w_bf, tm=1024, tn=1024):
    m, k = a.shape
    n = w_bf.shape[1]
    return pl.pallas_call(
        _up_kernel,
        grid=(n // tn, m // tm),
        in_specs=[pl.BlockSpec((tm, k), lambda j, i: (i, 0)),
                  pl.BlockSpec((k, tn), lambda j, i: (0, j))],
        out_specs=pl.BlockSpec((tm, tn), lambda j, i: (i, j)),
        out_shape=jax.ShapeDtypeStruct((m, n), BF16),
        compiler_params=_params(("parallel", "parallel")),
        name="mlp_up",
    )(a, w_bf)


def _down_kernel(u_ref, w_ref, h_ref, o_ref, ob_ref, acc_ref):
    kk = pl.program_id(2)

    @pl.when(kk == 0)
    def _():
        acc_ref[...] = h_ref[...]

    acc_ref[...] += jnp.dot(u_ref[...], w_ref[...], preferred_element_type=F32)

    @pl.when(kk == pl.num_programs(2) - 1)
    def _():
        h = acc_ref[...]
        o_ref[...] = h
        ob_ref[...] = h.astype(ob_ref.dtype)


def _mlp_down(u, w_bf, h2d, tm=1024, tn=1024, tk=2048):
    m, k = u.shape
    d = w_bf.shape[1]
    return pl.pallas_call(
        _down_kernel,
        grid=(d // tn, m // tm, k // tk),
        in_specs=[pl.BlockSpec((tm, tk), lambda j, i, kk: (i, kk)),
                  pl.BlockSpec((tk, tn), lambda j, i, kk: (kk, j)),
                  pl.BlockSpec((tm, tn), lambda j, i, kk: (i, j))],
        out_specs=[pl.BlockSpec((tm, tn), lambda j, i, kk: (i, j)),
                   pl.BlockSpec((tm, tn), lambda j, i, kk: (i, j))],
        out_shape=[jax.ShapeDtypeStruct((m, d), F32),
                   jax.ShapeDtypeStruct((m, d), BF16)],
        scratch_shapes=[pltpu.VMEM((tm, tn), F32)],
        compiler_params=_params(("parallel", "parallel", "arbitrary")),
        name="mlp_down",
    )(u, w_bf, h2d)


def _ple_kernel(p_ref, w_ref, g_ref, o_ref):
    y = jnp.dot(p_ref[...].astype(BF16), w_ref[...], preferred_element_type=F32)
    ms = jnp.mean(y * y, axis=-1, keepdims=True)
    o_ref[...] = y * lax.rsqrt(ms + NORM_EPS) * g_ref[...]


def _ple_embed(p2d, w_bf, g, tm=512):
    m, k = p2d.shape
    d = w_bf.shape[1]
    return pl.pallas_call(
        _ple_kernel,
        grid=(m // tm,),
        in_specs=[pl.BlockSpec((tm, k), lambda i: (i, 0)),
                  pl.BlockSpec((k, d), lambda i: (0, 0)),
                  pl.BlockSpec((1, d), lambda i: (0, 0))],
        out_specs=pl.BlockSpec((tm, d), lambda i: (i, 0)),
        out_shape=jax.ShapeDtypeStruct((m, d), F32),
        compiler_params=_params(("parallel",)),
        name="ple_embed",
    )(p2d, w_bf, g.reshape(1, d))


def _gate_kernel(hb_ref, w_ref, h_ref, pe_ref, o_ref):
    z = jnp.dot(hb_ref[...], w_ref[...], preferred_element_type=F32)
    o_ref[...] = h_ref[...] + jax.nn.sigmoid(z) * pe_ref[...]


def _ple_gate(hb, w_bf, h2d, pe, tm=512, tn=1024):
    m, d = h2d.shape
    tile = pl.BlockSpec((tm, tn), lambda j, i: (i, j))
    return pl.pallas_call(
        _gate_kernel,
        grid=(d // tn, m // tm),
        in_specs=[pl.BlockSpec((tm, d), lambda j, i: (i, 0)),
                  pl.BlockSpec((d, tn), lambda j, i: (0, j)),
                  tile, tile],
        out_specs=tile,
        out_shape=jax.ShapeDtypeStruct((m, d), F32),
        compiler_params=_params(("parallel", "parallel")),
        name="ple_gate",
    )(hb, w_bf, h2d, pe)


def _rope_tables(seq):
    pos = jnp.arange(seq, dtype=F32)

    def tab(dim):
        inv = 1.0 / (ROPE_THETA ** (jnp.arange(0, dim, 2, dtype=F32) / dim))
        ang = pos[:, None] * inv[None, :]
        ang = jnp.concatenate([ang, ang], axis=-1)
        return jnp.cos(ang), jnp.sin(ang)

    lane = jnp.arange(LANES)
    cos_a, sin_a = tab(HEAD_DIM)
    sin_a = jnp.where(lane < HEAD_DIM // 2, -sin_a, sin_a)
    cos_b, sin_b = tab(DIFF_QK_DIM)
    cos_b = jnp.concatenate([cos_b, cos_b], axis=-1)
    sin_b = jnp.concatenate([sin_b, sin_b], axis=-1)
    first_half = (lane % DIFF_QK_DIM) < DIFF_QK_DIM // 2
    sin_b_lo = jnp.where(first_half, -sin_b, 0.0)
    sin_b_hi = jnp.where(first_half, 0.0, sin_b)
    return cos_a, sin_a, cos_b, sin_b_lo, sin_b_hi


def _gain_rows(swa_q_g, swa_k_g, diff_q_g, diff_k_g):
    ones = jnp.ones((IN_TN,), F32)
    qa = jnp.tile(swa_q_g, IN_TN // HEAD_DIM) * (LOG2E / math.sqrt(HEAD_DIM))
    ka = jnp.tile(swa_k_g, IN_TN // HEAD_DIM)
    qb = jnp.tile(diff_q_g, IN_TN // DIFF_QK_DIM) * (LOG2E / math.sqrt(DIFF_QK_DIM))
    kb = jnp.tile(diff_k_g, IN_TN // DIFF_QK_DIM)
    rows = ([qa] * (SWA_Q_COLS // IN_TN) + [ka] * (SWA_KV_COLS // IN_TN)
            + [ones] * (SWA_KV_COLS // IN_TN)
            + [qb] * (DIFF_QK_COLS // IN_TN) + [kb] * (DIFF_QK_COLS // IN_TN)
            + [ones] * (DIFF_V_COLS // IN_TN))
    return jnp.stack(rows).reshape(IN_TILES, 1, IN_TN)


def _scores_bounded(q_g, k_g, dim, sinks=None):
    rounding_slack = 1.02
    bound = (math.sqrt(dim) * LOG2E * rounding_slack
             * jnp.max(jnp.abs(q_g)) * jnp.max(jnp.abs(k_g)))
    if sinks is not None:
        bound = jnp.maximum(bound, LOG2E * jnp.max(jnp.abs(sinks)))
    return (bound <= SCORE_BOUND).astype(jnp.int32).reshape(1)


def kernel(x, p, attn_norm_g, w_in, swa_q_norm_g, swa_k_norm_g, swa_sinks, diff_q_norm_g, diff_k_norm_g, diff_lambda_q1, diff_lambda_k1, diff_lambda_q2, diff_lambda_k2, diff_subln_g, w_o, mlp_norm_g, w_up, w_down, w_ple_proj, ple_norm_g, w_ple_gate):
    batch, seq, d = x.shape
    depth = w_in.shape[0]
    tables = _rope_tables(seq)
    h = x.reshape(batch * seq, d)
    for i in range(depth):
        lambda_init = 0.8 - 0.6 * math.exp(-0.3 * i)
        a = _rmsnorm_rows(h, attn_norm_g[i])
        gains = _gain_rows(swa_q_norm_g[i], swa_k_norm_g[i], diff_q_norm_g[i], diff_k_norm_g[i])
        proj = _in_proj(a, w_in[i].astype(BF16), gains, tables, seq)
        bounded_a = _scores_bounded(swa_q_norm_g[i], swa_k_norm_g[i], HEAD_DIM, swa_sinks[i])
        out_a = _swa_attention(proj, bounded_a, swa_sinks[i], batch, seq)
        bounded_b = _scores_bounded(diff_q_norm_g[i], diff_k_norm_g[i], DIFF_QK_DIM)
        out_b = _diff_attention(proj, bounded_b, diff_lambda_q1[i], diff_lambda_k1[i], diff_lambda_q2[i],
                                diff_lambda_k2[i], diff_subln_g[i], batch, seq, lambda_init)
        h = _out_proj(out_a, out_b, w_o[i].astype(BF16), h)
        m = _rmsnorm_rows(h, mlp_norm_g[i])
        u = _mlp_up(m, w_up[i].astype(BF16))
        h, hb = _mlp_down(u, w_down[i].astype(BF16), h)
        pe = _ple_embed(p[i].reshape(batch * seq, PLE_DIM), w_ple_proj[i].astype(BF16), ple_norm_g[i])
        h = _ple_gate(hb, w_ple_gate[i].astype(BF16), h, pe)
    return h.reshape(batch, seq, d)
```

```python
import functools
import math

import jax
import jax.numpy as jnp
from jax import lax
from jax.experimental import pallas as pl
from jax.experimental.pallas import tpu as pltpu

F32 = jnp.float32
BF16 = jnp.bfloat16

D_MODEL = 4096
HEAD_DIM = 128
SWA_HEADS = 16
SWA_KV_HEADS = 4
SWA_GROUP = SWA_HEADS // SWA_KV_HEADS
BLOCK = 128
DIFF_HEADS = 16
DIFF_QK_DIM = 64
D_FF = 4 * D_MODEL
PLE_DIM = 256
ROPE_THETA = 10000.0
NORM_EPS = 1e-6
NEG_INF = -1e30
LOG2E = math.log2(math.e)
SCORE_BOUND = 40.0

SWA_Q_COLS = SWA_HEADS * HEAD_DIM
SWA_KV_COLS = SWA_KV_HEADS * HEAD_DIM
DIFF_QK_COLS = 2 * DIFF_HEADS * DIFF_QK_DIM
DIFF_V_COLS = DIFF_HEADS * HEAD_DIM
IN_COLS = SWA_Q_COLS + 2 * SWA_KV_COLS + 2 * DIFF_QK_COLS + DIFF_V_COLS
MIX_HALF = SWA_Q_COLS

LANES = 128
VMEM_LIMIT = 56 * 1024 * 1024

KA_BLK = SWA_Q_COLS // LANES
VA_BLK = KA_BLK + SWA_KV_COLS // LANES
QB_BLK = VA_BLK + SWA_KV_COLS // LANES
KB_BLK = QB_BLK + DIFF_QK_COLS // LANES
VB_BLK = KB_BLK + DIFF_QK_COLS // LANES

IN_TN = 512
IN_TILES = IN_COLS // IN_TN
IN_SUB_ROWS = 256
_A_END = (SWA_Q_COLS + SWA_KV_COLS) // IN_TN
_VA_TILE = _A_END
_B_START = _VA_TILE + 1
_B_END = _B_START + 2 * DIFF_QK_COLS // IN_TN


def _params(sem):
    return pltpu.CompilerParams(dimension_semantics=sem, vmem_limit_bytes=VMEM_LIMIT)


def _cast_rider(w, grid):
    steps = math.prod(grid)
    rows = w.shape[0] // steps
    assert rows * steps == w.shape[0] and rows % 16 == 0, (w.shape, grid)

    def index_map(*idx):
        step = idx[0]
        for size, i in zip(grid[1:], idx[1:]):
            step = step * size + i
        return (step, 0)

    spec = pl.BlockSpec((rows, w.shape[1]), index_map)
    return spec, spec, jax.ShapeDtypeStruct(w.shape, BF16)


def _cast_chunk(wf_ref, wbf_ref):
    wbf_ref[...] = wf_ref[...].astype(BF16)


def _rmsnorm_kernel(x_ref, g_ref, o_ref):
    x = x_ref[...]
    ms = jnp.mean(x * x, axis=-1, keepdims=True)
    o_ref[...] = (x * lax.rsqrt(ms + NORM_EPS) * g_ref[...]).astype(o_ref.dtype)


def _rmsnorm_rows(x2d, g, tm=512):
    m, d = x2d.shape
    return pl.pallas_call(
        _rmsnorm_kernel,
        grid=(m // tm,),
        in_specs=[pl.BlockSpec((tm, d), lambda i: (i, 0)),
                  pl.BlockSpec((1, d), lambda i: (0, 0))],
        out_specs=pl.BlockSpec((tm, d), lambda i: (i, 0)),
        out_shape=jax.ShapeDtypeStruct((m, d), BF16),
        compiler_params=_params(("parallel",)),
        name="rmsnorm_rows",
    )(x2d, g.reshape(1, d))


def _in_proj_kernel(a_ref, wf_ref, g_ref, ca_ref, sa_ref, cb_ref, sbl_ref, sbh_ref, o_ref, w_ref):
    n = pl.program_id(0)
    nsub = IN_TN // LANES
    tm = a_ref.shape[0]

    @pl.when(pl.program_id(1) == 0)
    def _():
        w_ref[...] = wf_ref[...].astype(BF16)

    def run(epilogue):
        for r in range(tm // IN_SUB_ROWS):
            rs = slice(r * IN_SUB_ROWS, (r + 1) * IN_SUB_ROWS)
            acc = jnp.dot(a_ref[rs, :], w_ref[...], preferred_element_type=F32)
            epilogue(acc, rs)

    def plain(acc, rs):
        o_ref[rs, :] = acc.astype(o_ref.dtype)

    def heads128(acc, rs):
        for c in range(nsub):
            sl = slice(c * LANES, (c + 1) * LANES)
            y = acc[:, sl]
            ms = jnp.mean(y * y, axis=-1, keepdims=True)
            yn = y * lax.rsqrt(ms + NORM_EPS) * g_ref[0][:, sl]
            out = yn * ca_ref[rs, :] + pltpu.roll(yn, HEAD_DIM // 2, 1) * sa_ref[rs, :]
            o_ref[rs, sl] = out.astype(o_ref.dtype)

    def heads64(acc, rs):
        lane = lax.broadcasted_iota(jnp.int32, (1, LANES), 1)
        lo = lane < DIFF_QK_DIM
        for c in range(nsub):
            sl = slice(c * LANES, (c + 1) * LANES)
            y = acc[:, sl]
            sq = y * y
            s_lo = jnp.sum(jnp.where(lo, sq, 0.0), axis=-1, keepdims=True)
            s_hi = jnp.sum(jnp.where(lo, 0.0, sq), axis=-1, keepdims=True)
            ms = jnp.where(lo, s_lo, s_hi) * (1.0 / DIFF_QK_DIM)
            yn = y * lax.rsqrt(ms + NORM_EPS) * g_ref[0][:, sl]
            out = (yn * cb_ref[rs, :]
                   + pltpu.roll(yn, LANES - DIFF_QK_DIM // 2, 1) * sbl_ref[rs, :]
                   + pltpu.roll(yn, DIFF_QK_DIM // 2, 1) * sbh_ref[rs, :])
            o_ref[rs, sl] = out.astype(o_ref.dtype)

    is_a = n < _A_END
    is_b = (n >= _B_START) & (n < _B_END)
    pl.when(is_a)(lambda: run(heads128))
    pl.when(is_b)(lambda: run(heads64))
    pl.when(jnp.logical_not(is_a | is_b))(lambda: run(plain))


def _in_proj(a, w, gains, tables, seq, tm=1024):
    m, k = a.shape
    pos_blocks = seq // tm
    tab_spec = pl.BlockSpec((tm, LANES), lambda n, i: (i % pos_blocks, 0))
    return pl.pallas_call(
        _in_proj_kernel,
        grid=(IN_TILES, m // tm),
        in_specs=[pl.BlockSpec((tm, k), lambda n, i: (i, 0)),
                  pl.BlockSpec((k, IN_TN), lambda n, i: (0, n)),
                  pl.BlockSpec((1, 1, IN_TN), lambda n, i: (n, 0, 0)),
                  tab_spec, tab_spec, tab_spec, tab_spec, tab_spec],
        out_specs=pl.BlockSpec((tm, IN_TN), lambda n, i: (i, n)),
        out_shape=jax.ShapeDtypeStruct((m, IN_COLS), BF16),
        scratch_shapes=[pltpu.VMEM((k, IN_TN), BF16)],
        compiler_params=_params(("parallel", "arbitrary")),
        name="in_proj",
    )(a, w, gains, *tables)


def _swa_kernel(bounded_ref, sink_ref, sinkrow_ref, q_ref, kp_ref, kc_ref, vp_ref, vc_ref, wf_ref,
                o_ref, wbf_ref):
    _cast_chunk(wf_ref, wbf_ref)
    i = pl.program_id(1)
    nt = (((1,), (1,)), ((), ()))
    rows = SWA_GROUP * BLOCK
    r = lax.broadcasted_iota(jnp.int32, (rows, 2 * BLOCK), 0) & (BLOCK - 1)
    c = lax.broadcasted_iota(jnp.int32, (rows, 2 * BLOCK), 1)
    valid = (c > r) & (c <= r + BLOCK) & ((i > 0) | (c >= BLOCK))
    ones = jnp.ones((2 * BLOCK, HEAD_DIM), BF16)

    def masked_scores(h):
        ks = slice(h * HEAD_DIM, (h + 1) * HEAD_DIM)
        kk = jnp.concatenate([kp_ref[:, ks], kc_ref[:, ks]], axis=0)
        vv = jnp.concatenate([vp_ref[:, ks], vc_ref[:, ks]], axis=0)
        heads = [h * SWA_GROUP + g for g in range(SWA_GROUP)]
        q4 = jnp.concatenate([q_ref[:, hd * HEAD_DIM:(hd + 1) * HEAD_DIM] for hd in heads], axis=0)
        s = lax.dot_general(q4, kk, nt, preferred_element_type=F32)
        return jnp.where(valid, s, NEG_INF), vv, heads

    @pl.when(bounded_ref[0] != 0)
    def _():
        for h in range(SWA_KV_HEADS):
            s, vv, heads = masked_scores(h)
            vaug = jnp.concatenate([vv, ones], axis=1)
            o = jnp.dot(jnp.exp2(s).astype(BF16), vaug, preferred_element_type=F32)
            for g, hd in enumerate(heads):
                rs = slice(g * BLOCK, (g + 1) * BLOCK)
                sink = jnp.exp2(sinkrow_ref[hd:hd + 1, :] * LOG2E)
                out = o[rs, :HEAD_DIM] / (o[rs, HEAD_DIM:] + sink)
                o_ref[:, hd * HEAD_DIM:(hd + 1) * HEAD_DIM] = out.astype(o_ref.dtype)

    @pl.when(bounded_ref[0] == 0)
    def _():
        for h in range(SWA_KV_HEADS):
            s, vv, heads = masked_scores(h)
            for g, hd in enumerate(heads):
                sg = s[g * BLOCK:(g + 1) * BLOCK]
                sink = sink_ref[hd] * LOG2E
                mx = jnp.maximum(jnp.max(sg, axis=-1, keepdims=True), sink)
                p = jnp.exp2(sg - mx)
                denom = jnp.sum(p, axis=-1, keepdims=True) + jnp.exp2(sink - mx)
                o = jnp.dot(p.astype(BF16), vv, preferred_element_type=F32)
                o_ref[:, hd * HEAD_DIM:(hd + 1) * HEAD_DIM] = (o / denom).astype(o_ref.dtype)


def _swa_attention(proj, bounded, sinks, w_next, batch, seq):
    nb = seq // BLOCK
    w_in_spec, w_out_spec, w_out_shape = _cast_rider(w_next, (batch, nb))
    qw = SWA_HEADS * HEAD_DIM
    kw = SWA_KV_HEADS * HEAD_DIM
    row = lambda b, i: b * nb + i
    prev = lambda b, i: b * nb + jnp.maximum(i - 1, 0)
    k_blk = SWA_Q_COLS // kw
    v_blk = k_blk + 1
    sink_rows = jnp.broadcast_to(sinks[:, None], (SWA_HEADS, HEAD_DIM))
    return pl.pallas_call(
        _swa_kernel,
        grid=(batch, nb),
        in_specs=[pl.BlockSpec(memory_space=pltpu.SMEM),
                  pl.BlockSpec(memory_space=pltpu.SMEM),
                  pl.BlockSpec((SWA_HEADS, HEAD_DIM), lambda b, i: (0, 0)),
                  pl.BlockSpec((BLOCK, qw), lambda b, i: (row(b, i), 0)),
                  pl.BlockSpec((BLOCK, kw), lambda b, i: (prev(b, i), k_blk)),
                  pl.BlockSpec((BLOCK, kw), lambda b, i: (row(b, i), k_blk)),
                  pl.BlockSpec((BLOCK, kw), lambda b, i: (prev(b, i), v_blk)),
                  pl.BlockSpec((BLOCK, kw), lambda b, i: (row(b, i), v_blk)),
                  w_in_spec],
        out_specs=[pl.BlockSpec((BLOCK, qw), lambda b, i: (row(b, i), 0)), w_out_spec],
        out_shape=[jax.ShapeDtypeStruct((batch * seq, MIX_HALF), BF16), w_out_shape],
        compiler_params=_params(("parallel", "parallel")),
        name="swa_attention",
    )(bounded, sinks, sink_rows, proj, proj, proj, proj, proj, w_next)


DIFF_TQ = 512
DIFF_TK = 256
DIFF_DIAG = DIFF_TQ // DIFF_TK
assert DIFF_DIAG * DIFF_TK == DIFF_TQ and DIFF_DIAG % 2 == 0


def _diff_kernel(bounded_ref, lq1_ref, lk1_ref, lq2_ref, lk2_ref, subg_ref,
                 q1_ref, q2_ref, k1_ref, k2_ref, v_ref, wf_ref, o_ref, wbf_ref,
                 vaug_ref, acc_ref, m_ref, l_ref, accs_ref, *, lambda_init):
    _cast_chunk(wf_ref, wbf_ref)
    qi = pl.program_id(2)
    tq = DIFF_TQ
    nt = (((1,), (1,)), ((), ()))
    lane = lax.broadcasted_iota(jnp.int32, (tq, LANES), 1)
    qs = []
    for q_ref in (q1_ref, q2_ref):
        q = q_ref[...]
        for e in range(2):
            keep = (lane >= e * DIFF_QK_DIM) & (lane < (e + 1) * DIFF_QK_DIM)
            qs.append(jnp.where(keep, q, jnp.zeros_like(q)))

    tk = DIFF_TK
    k_refs = (k1_ref, k2_ref)
    row = lax.broadcasted_iota(jnp.int32, (tq, tk), 0)
    col = lax.broadcasted_iota(jnp.int32, (tq, tk), 1)
    causal = col <= row

    def scores(slot, j, r0, masked):
        off = pl.multiple_of(j * tk, tk)
        s = lax.dot_general(qs[slot][r0:], k_refs[slot // 2][pl.ds(off, tk), :], nt,
                            preferred_element_type=F32)
        if masked:
            s = jnp.where(causal[:tq - r0], s, NEG_INF)
        return s

    lam = (jnp.exp(jnp.sum(lq1_ref[...] * lk1_ref[...], axis=-1, keepdims=True))
           - jnp.exp(jnp.sum(lq2_ref[...] * lk2_ref[...], axis=-1, keepdims=True))
           + lambda_init)

    def finalize(e, o1, o2):
        o = o1 - lam * o2
        ms = jnp.mean(o * o, axis=-1, keepdims=True)
        on = o * lax.rsqrt(ms + NORM_EPS) * subg_ref[...]
        o_ref[:, e * HEAD_DIM:(e + 1) * HEAD_DIM] = (on * (1.0 - lambda_init)).astype(o_ref.dtype)

    @pl.when(qi == 0)
    def _():
        for e in range(2):
            vaug_ref[e, :, :HEAD_DIM] = v_ref[:, e * HEAD_DIM:(e + 1) * HEAD_DIM]
            vaug_ref[e, :, HEAD_DIM:] = jnp.ones((v_ref.shape[0], HEAD_DIM), BF16)

    @pl.when(bounded_ref[0] != 0)
    def _():
        def step(j, r0, masked, assign=False):
            off = pl.multiple_of(j * tk, tk)
            ps = [jnp.exp2(scores(slot, j, r0, masked)).astype(BF16) for slot in range(4)]
            for e in range(2):
                pp = jnp.concatenate([ps[e], ps[2 + e]], axis=0)
                pv = jnp.dot(pp, vaug_ref[e, pl.ds(off, tk), :], preferred_element_type=F32)
                if assign:
                    acc_ref[e] = pv
                else:
                    acc_ref[e, r0:tq] += pv[:tq - r0]
                    acc_ref[e, tq + r0:] += pv[tq - r0:]

        step(DIFF_DIAG * qi, 0, True, assign=True)
        for d in range(1, DIFF_DIAG):
            step(DIFF_DIAG * qi + d, d * tk, True)

        def body(jj, carry):
            step(2 * jj, 0, False)
            step(2 * jj + 1, 0, False)
            return carry

        lax.fori_loop(0, (DIFF_DIAG * qi) // 2, body, 0)
        for e in range(2):
            a = acc_ref[e]
            finalize(e, a[:tq, :HEAD_DIM] / a[:tq, HEAD_DIM:],
                     a[tq:, :HEAD_DIM] / a[tq:, HEAD_DIM:])

    @pl.when(bounded_ref[0] == 0)
    def _():
        m_ref[...] = jnp.full(m_ref.shape, NEG_INF, F32)
        l_ref[...] = jnp.zeros(l_ref.shape, F32)
        accs_ref[...] = jnp.zeros(accs_ref.shape, F32)

        def step(j, r0, masked):
            off = pl.multiple_of(j * tk, tk)
            vb = v_ref[pl.ds(off, tk), :]
            for slot in range(4):
                e = slot % 2
                s = scores(slot, j, r0, masked)
                m_prev = m_ref[slot, r0:]
                m_new = jnp.maximum(m_prev, jnp.max(s, axis=-1, keepdims=True))
                alpha = jnp.exp2(m_prev - m_new)
                p = jnp.exp2(s - m_new)
                l_ref[slot, r0:] = alpha * l_ref[slot, r0:] + jnp.sum(p, axis=-1, keepdims=True)
                pv = jnp.dot(p.astype(BF16), vb[:, e * HEAD_DIM:(e + 1) * HEAD_DIM],
                             preferred_element_type=F32)
                accs_ref[slot, r0:] = alpha * accs_ref[slot, r0:] + pv
                m_ref[slot, r0:] = m_new

        def body(j, carry):
            step(j, 0, False)
            return carry

        lax.fori_loop(0, DIFF_DIAG * qi, body, 0)
        for d in range(DIFF_DIAG):
            step(DIFF_DIAG * qi + d, d * tk, True)
        for e in range(2):
            finalize(e, accs_ref[e] / l_ref[e], accs_ref[2 + e] / l_ref[2 + e])


def _diff_attention(proj, bounded, lq1, lk1, lq2, lk2, subg, w_next, batch, seq, lambda_init):
    tq = DIFF_TQ
    nq = seq // tq
    pairs = DIFF_HEADS // 2
    w_in_spec, w_out_spec, w_out_shape = _cast_rider(w_next, (batch, pairs, nq))
    map_blks = DIFF_HEADS * DIFF_QK_DIM // LANES
    vec = lambda d: pl.BlockSpec((1, d), lambda b, h, i: (0, 0))
    return pl.pallas_call(
        functools.partial(_diff_kernel, lambda_init=lambda_init),
        grid=(batch, pairs, nq),
        in_specs=[pl.BlockSpec(memory_space=pltpu.SMEM),
                  vec(DIFF_QK_DIM), vec(DIFF_QK_DIM), vec(DIFF_QK_DIM), vec(DIFF_QK_DIM),
                  vec(HEAD_DIM),
                  pl.BlockSpec((tq, LANES), lambda b, h, i: (b * nq + i, QB_BLK + h)),
                  pl.BlockSpec((tq, LANES), lambda b, h, i: (b * nq + i, QB_BLK + map_blks + h)),
                  pl.BlockSpec((seq, LANES), lambda b, h, i: (b, KB_BLK + h)),
                  pl.BlockSpec((seq, LANES), lambda b, h, i: (b, KB_BLK + map_blks + h)),
                  pl.BlockSpec((seq, 2 * HEAD_DIM), lambda b, h, i: (b, VB_BLK // 2 + h)),
                  w_in_spec],
        out_specs=[pl.BlockSpec((tq, 2 * HEAD_DIM), lambda b, h, i: (b * nq + i, h)), w_out_spec],
        out_shape=[jax.ShapeDtypeStruct((batch * seq, MIX_HALF), BF16), w_out_shape],
        scratch_shapes=[pltpu.VMEM((2, seq, 2 * HEAD_DIM), BF16),
                        pltpu.VMEM((2, 2 * tq, 2 * HEAD_DIM), F32),
                        pltpu.VMEM((4, tq, 1), F32),
                        pltpu.VMEM((4, tq, 1), F32),
                        pltpu.VMEM((4, tq, HEAD_DIM), F32)],
        compiler_params=_params(("parallel", "parallel", "arbitrary")),
        name="diff_attention",
    )(bounded, lq1.reshape(1, -1), lk1.reshape(1, -1), lq2.reshape(1, -1), lk2.reshape(1, -1),
      subg.reshape(1, -1), proj, proj, proj, proj, proj, w_next)


def _out_proj_kernel(oa_ref, ob_ref, w_ref, x_ref, o_ref):
    acc = jnp.dot(oa_ref[...], w_ref[:MIX_HALF, :], preferred_element_type=F32)
    acc += jnp.dot(ob_ref[...], w_ref[MIX_HALF:, :], preferred_element_type=F32)
    o_ref[...] = x_ref[...] + acc


def _out_proj(oa, ob, w_bf, x2d, tm=512, tn=1024):
    m, d = x2d.shape
    return pl.pallas_call(
        _out_proj_kernel,
        grid=(d // tn, m // tm),
        in_specs=[pl.BlockSpec((tm, MIX_HALF), lambda n, i: (i, 0)),
                  pl.BlockSpec((tm, MIX_HALF), lambda n, i: (i, 0)),
                  pl.BlockSpec((2 * MIX_HALF, tn), lambda n, i: (0, n)),
                  pl.BlockSpec((tm, tn), lambda n, i: (i, n))],
        out_specs=pl.BlockSpec((tm, tn), lambda n, i: (i, n)),
        out_shape=jax.ShapeDtypeStruct((m, d), F32),
        compiler_params=_params(("parallel", "parallel")),
        name="out_proj",
    )(oa, ob, w_bf, x2d)


def _up_kernel(a_ref, w_ref, wf1_ref, wf2_ref, o_ref, wbf1_ref, wbf2_ref):
    _cast_chunk(wf1_ref, wbf1_ref)
    _cast_chunk(wf2_ref, wbf2_ref)
    acc = jnp.dot(a_ref[...], w_ref[...], preferred_element_type=F32)
    o_ref[...] = jnp.square(jnp.maximum(acc, 0.0)).astype(o_ref.dtype)


def _mlp_up(a, w_bf, w_next1, w_next2, tm=1024, tn=1024):
    m, k = a.shape
    n = w_bf.shape[1]
    grid = (n // tn, m // tm)
    in1, out1, shape1 = _cast_rider(w_next1, grid)
    in2, out2, shape2 = _cast_rider(w_next2, grid)
    return pl.pallas_call(
        _up_kernel,
        grid=grid,
        in_specs=[pl.BlockSpec((tm, k), lambda j, i: (i, 0)),
                  pl.BlockSpec((k, tn), lambda j, i: (0, j)),
                  in1, in2],
        out_specs=[pl.BlockSpec((tm, tn), lambda j, i: (i, j)), out1, out2],
        out_shape=[jax.ShapeDtypeStruct((m, n), BF16), shape1, shape2],
        compiler_params=_params(("parallel", "parallel")),
        name="mlp_up",
    )(a, w_bf, w_next1, w_next2)


def _down_kernel(u_ref, w_ref, h_ref, o_ref, ob_ref, acc_ref):
    kk = pl.program_id(2)

    @pl.when(kk == 0)
    def _():
        acc_ref[...] = h_ref[...]

    acc_ref[...] += jnp.dot(u_ref[...], w_ref[...], preferred_element_type=F32)

    @pl.when(kk == pl.num_programs(2) - 1)
    def _():
        h = acc_ref[...]
        o_ref[...] = h
        ob_ref[...] = h.astype(ob_ref.dtype)


def _mlp_down(u, w_bf, h2d, tm=1024, tn=1024, tk=2048):
    m, k = u.shape
    d = w_bf.shape[1]
    return pl.pallas_call(
        _down_kernel,
        grid=(d // tn, m // tm, k // tk),
        in_specs=[pl.BlockSpec((tm, tk), lambda j, i, kk: (i, kk)),
                  pl.BlockSpec((tk, tn), lambda j, i, kk: (kk, j)),
                  pl.BlockSpec((tm, tn), lambda j, i, kk: (i, j))],
        out_specs=[pl.BlockSpec((tm, tn), lambda j, i, kk: (i, j)),
                   pl.BlockSpec((tm, tn), lambda j, i, kk: (i, j))],
        out_shape=[jax.ShapeDtypeStruct((m, d), F32),
                   jax.ShapeDtypeStruct((m, d), BF16)],
        scratch_shapes=[pltpu.VMEM((tm, tn), F32)],
        compiler_params=_params(("parallel", "parallel", "arbitrary")),
        name="mlp_down",
    )(u, w_bf, h2d)


def _ple_kernel(p_ref, w_ref, g_ref, o_ref):
    y = jnp.dot(p_ref[...].astype(BF16), w_ref[...], preferred_element_type=F32)
    ms = jnp.mean(y * y, axis=-1, keepdims=True)
    o_ref[...] = y * lax.rsqrt(ms + NORM_EPS) * g_ref[...]


def _ple_embed(p2d, w_bf, g, tm=512):
    m, k = p2d.shape
    d = w_bf.shape[1]
    return pl.pallas_call(
        _ple_kernel,
        grid=(m // tm,),
        in_specs=[pl.BlockSpec((tm, k), lambda i: (i, 0)),
                  pl.BlockSpec((k, d), lambda i: (0, 0)),
                  pl.BlockSpec((1, d), lambda i: (0, 0))],
        out_specs=pl.BlockSpec((tm, d), lambda i: (i, 0)),
        out_shape=jax.ShapeDtypeStruct((m, d), F32),
        compiler_params=_params(("parallel",)),
        name="ple_embed",
    )(p2d, w_bf, g.reshape(1, d))


def _gate_kernel(hb_ref, w_ref, h_ref, pe_ref, o_ref):
    z = jnp.dot(hb_ref[...], w_ref[...], preferred_element_type=F32)
    o_ref[...] = h_ref[...] + jax.nn.sigmoid(z) * pe_ref[...]


def _ple_gate(hb, w_bf, h2d, pe, tm=512, tn=1024):
    m, d = h2d.shape
    tile = pl.BlockSpec((tm, tn), lambda j, i: (i, j))
    return pl.pallas_call(
        _gate_kernel,
        grid=(d // tn, m // tm),
        in_specs=[pl.BlockSpec((tm, d), lambda j, i: (i, 0)),
                  pl.BlockSpec((d, tn), lambda j, i: (0, j)),
                  tile, tile],
        out_specs=tile,
        out_shape=jax.ShapeDtypeStruct((m, d), F32),
        compiler_params=_params(("parallel", "parallel")),
        name="ple_gate",
    )(hb, w_bf, h2d, pe)


def _rope_tables(seq):
    pos = jnp.arange(seq, dtype=F32)

    def tab(dim):
        inv = 1.0 / (ROPE_THETA ** (jnp.arange(0, dim, 2, dtype=F32) / dim))
        ang = pos[:, None] * inv[None, :]
        ang = jnp.concatenate([ang, ang], axis=-1)
        return jnp.cos(ang), jnp.sin(ang)

    lane = jnp.arange(LANES)
    cos_a, sin_a = tab(HEAD_DIM)
    sin_a = jnp.where(lane < HEAD_DIM // 2, -sin_a, sin_a)
    cos_b, sin_b = tab(DIFF_QK_DIM)
    cos_b = jnp.concatenate([cos_b, cos_b], axis=-1)
    sin_b = jnp.concatenate([sin_b, sin_b], axis=-1)
    first_half = (lane % DIFF_QK_DIM) < DIFF_QK_DIM // 2
    sin_b_lo = jnp.where(first_half, -sin_b, 0.0)
    sin_b_hi = jnp.where(first_half, 0.0, sin_b)
    return cos_a, sin_a, cos_b, sin_b_lo, sin_b_hi


def _gain_rows(swa_q_g, swa_k_g, diff_q_g, diff_k_g):
    ones = jnp.ones((IN_TN,), F32)
    qa = jnp.tile(swa_q_g, IN_TN // HEAD_DIM) * (LOG2E / math.sqrt(HEAD_DIM))
    ka = jnp.tile(swa_k_g, IN_TN // HEAD_DIM)
    qb = jnp.tile(diff_q_g, IN_TN // DIFF_QK_DIM) * (LOG2E / math.sqrt(DIFF_QK_DIM))
    kb = jnp.tile(diff_k_g, IN_TN // DIFF_QK_DIM)
    rows = ([qa] * (SWA_Q_COLS // IN_TN) + [ka] * (SWA_KV_COLS // IN_TN)
            + [ones] * (SWA_KV_COLS // IN_TN)
            + [qb] * (DIFF_QK_COLS // IN_TN) + [kb] * (DIFF_QK_COLS // IN_TN)
            + [ones] * (DIFF_V_COLS // IN_TN))
    return jnp.stack(rows).reshape(IN_TILES, 1, IN_TN)


def _scores_bounded(q_g, k_g, dim, sinks=None):
    rounding_slack = 1.02
    bound = (math.sqrt(dim) * LOG2E * rounding_slack
             * jnp.max(jnp.abs(q_g)) * jnp.max(jnp.abs(k_g)))
    if sinks is not None:
        bound = jnp.maximum(bound, LOG2E * jnp.max(jnp.abs(sinks)))
    return (bound <= SCORE_BOUND).astype(jnp.int32).reshape(1)


def kernel(x, p, attn_norm_g, w_in, swa_q_norm_g, swa_k_norm_g, swa_sinks, diff_q_norm_g, diff_k_norm_g, diff_lambda_q1, diff_lambda_k1, diff_lambda_q2, diff_lambda_k2, diff_subln_g, w_o, mlp_norm_g, w_up, w_down, w_ple_proj, ple_norm_g, w_ple_gate):
    batch, seq, d = x.shape
    depth = w_in.shape[0]
    tables = _rope_tables(seq)
    h = x.reshape(batch * seq, d)
    for i in range(depth):
        lambda_init = 0.8 - 0.6 * math.exp(-0.3 * i)
        a = _rmsnorm_rows(h, attn_norm_g[i])
        gains = _gain_rows(swa_q_norm_g[i], swa_k_norm_g[i], diff_q_norm_g[i], diff_k_norm_g[i])
        proj = _in_proj(a, w_in[i], gains, tables, seq)
        bounded_a = _scores_bounded(swa_q_norm_g[i], swa_k_norm_g[i], HEAD_DIM, swa_sinks[i])
        out_a, w_o_bf = _swa_attention(proj, bounded_a, swa_sinks[i], w_o[i], batch, seq)
        bounded_b = _scores_bounded(diff_q_norm_g[i], diff_k_norm_g[i], DIFF_QK_DIM)
        out_b, w_up_bf = _diff_attention(proj, bounded_b, diff_lambda_q1[i], diff_lambda_k1[i],
                                         diff_lambda_q2[i], diff_lambda_k2[i], diff_subln_g[i],
                                         w_up[i], batch, seq, lambda_init)
        h = _out_proj(out_a, out_b, w_o_bf, h)
        m = _rmsnorm_rows(h, mlp_norm_g[i])
        u, w_down_bf, w_gate_bf = _mlp_up(m, w_up_bf, w_down[i], w_ple_gate[i])
        h, hb = _mlp_down(u, w_down_bf, h)
        pe = _ple_embed(p[i].reshape(batch * seq, PLE_DIM), w_ple_proj[i].astype(BF16), ple_norm_g[i])
        h = _ple_gate(hb, w_gate_bf, h, pe)
    return h.reshape(batch, seq, d)
```

```python
import functools
import math

import jax
import jax.numpy as jnp
from jax import lax
from jax.experimental import pallas as pl
from jax.experimental.pallas import tpu as pltpu

F32 = jnp.float32
BF16 = jnp.bfloat16

D_MODEL = 4096
HEAD_DIM = 128
SWA_HEADS = 16
SWA_KV_HEADS = 4
SWA_GROUP = SWA_HEADS // SWA_KV_HEADS
BLOCK = 128
DIFF_HEADS = 16
DIFF_QK_DIM = 64
D_FF = 4 * D_MODEL
PLE_DIM = 256
ROPE_THETA = 10000.0
NORM_EPS = 1e-6
NEG_INF = -1e30
LOG2E = math.log2(math.e)
SCORE_BOUND = 40.0

SWA_Q_COLS = SWA_HEADS * HEAD_DIM
SWA_KV_COLS = SWA_KV_HEADS * HEAD_DIM
DIFF_QK_COLS = 2 * DIFF_HEADS * DIFF_QK_DIM
DIFF_V_COLS = DIFF_HEADS * HEAD_DIM
IN_COLS = SWA_Q_COLS + 2 * SWA_KV_COLS + 2 * DIFF_QK_COLS + DIFF_V_COLS
MIX_HALF = SWA_Q_COLS

LANES = 128
VMEM_LIMIT = 56 * 1024 * 1024

KA_BLK = SWA_Q_COLS // LANES
VA_BLK = KA_BLK + SWA_KV_COLS // LANES
QB_BLK = VA_BLK + SWA_KV_COLS // LANES
KB_BLK = QB_BLK + DIFF_QK_COLS // LANES
VB_BLK = KB_BLK + DIFF_QK_COLS // LANES

IN_TN = 1024
IN_TILES = IN_COLS // IN_TN
IN_HALF = IN_TN // 2
IN_SUB_ROWS = 128
_HALF_KINDS = (['a'] * ((SWA_Q_COLS + SWA_KV_COLS) // IN_HALF) + ['p'] * (SWA_KV_COLS // IN_HALF)
               + ['b'] * (2 * DIFF_QK_COLS // IN_HALF) + ['p'] * (DIFF_V_COLS // IN_HALF))
_TILE_KINDS = [(_HALF_KINDS[2 * t], _HALF_KINDS[2 * t + 1]) for t in range(IN_TILES)]


def _params(sem):
    return pltpu.CompilerParams(dimension_semantics=sem, vmem_limit_bytes=VMEM_LIMIT)


def _cast_rider(w, grid):
    steps = math.prod(grid)
    rows = w.shape[0] // steps
    assert rows * steps == w.shape[0] and rows % 16 == 0, (w.shape, grid)

    def index_map(*idx):
        step = idx[0]
        for size, i in zip(grid[1:], idx[1:]):
            step = step * size + i
        return (step, 0)

    spec = pl.BlockSpec((rows, w.shape[1]), index_map)
    return spec, spec, jax.ShapeDtypeStruct(w.shape, BF16)


def _cast_chunk(wf_ref, wbf_ref):
    wbf_ref[...] = wf_ref[...].astype(BF16)


def _rmsnorm_kernel(x_ref, g_ref, o_ref):
    x = x_ref[...]
    ms = jnp.mean(x * x, axis=-1, keepdims=True)
    o_ref[...] = (x * lax.rsqrt(ms + NORM_EPS) * g_ref[...]).astype(o_ref.dtype)


def _rmsnorm_rows(x2d, g, tm=512):
    m, d = x2d.shape
    return pl.pallas_call(
        _rmsnorm_kernel,
        grid=(m // tm,),
        in_specs=[pl.BlockSpec((tm, d), lambda i: (i, 0)),
                  pl.BlockSpec((1, d), lambda i: (0, 0))],
        out_specs=pl.BlockSpec((tm, d), lambda i: (i, 0)),
        out_shape=jax.ShapeDtypeStruct((m, d), BF16),
        compiler_params=_params(("parallel",)),
        name="rmsnorm_rows",
    )(x2d, g.reshape(1, d))


def _in_proj_kernel(a_ref, wf_ref, g_ref, ca_ref, sa_ref, cb_ref, sbl_ref, sbh_ref, o_ref, w_ref):
    r = pl.program_id(0)
    i = pl.program_id(1)
    tm = a_ref.shape[0]
    chunk = wf_ref.shape[0]

    @pl.when(r < IN_TILES)
    def _():
        w_ref[r % 2, pl.ds(pl.multiple_of(i * chunk, chunk), chunk), :] = wf_ref[...].astype(BF16)

    def plain(acc, rs, c0):
        o_ref[rs, c0:c0 + IN_HALF] = acc.astype(o_ref.dtype)

    def heads128(acc, rs, c0):
        for c in range(IN_HALF // LANES):
            sl = slice(c * LANES, (c + 1) * LANES)
            osl = slice(c0 + c * LANES, c0 + (c + 1) * LANES)
            y = acc[:, sl]
            ms = jnp.mean(y * y, axis=-1, keepdims=True)
            yn = y * lax.rsqrt(ms + NORM_EPS) * g_ref[0][:, osl]
            out = yn * ca_ref[rs, :] + pltpu.roll(yn, HEAD_DIM // 2, 1) * sa_ref[rs, :]
            o_ref[rs, osl] = out.astype(o_ref.dtype)

    def heads64(acc, rs, c0):
        lane = lax.broadcasted_iota(jnp.int32, (1, LANES), 1)
        lo = lane < DIFF_QK_DIM
        for c in range(IN_HALF // LANES):
            sl = slice(c * LANES, (c + 1) * LANES)
            osl = slice(c0 + c * LANES, c0 + (c + 1) * LANES)
            y = acc[:, sl]
            sq = y * y
            s_lo = jnp.sum(jnp.where(lo, sq, 0.0), axis=-1, keepdims=True)
            s_hi = jnp.sum(jnp.where(lo, 0.0, sq), axis=-1, keepdims=True)
            ms = jnp.where(lo, s_lo, s_hi) * (1.0 / DIFF_QK_DIM)
            yn = y * lax.rsqrt(ms + NORM_EPS) * g_ref[0][:, osl]
            out = (yn * cb_ref[rs, :]
                   + pltpu.roll(yn, LANES - DIFF_QK_DIM // 2, 1) * sbl_ref[rs, :]
                   + pltpu.roll(yn, DIFF_QK_DIM // 2, 1) * sbh_ref[rs, :])
            o_ref[rs, osl] = out.astype(o_ref.dtype)

    epilogues = {'a': heads128, 'b': heads64, 'p': plain}

    def run(kinds):
        slot = (r - 1) % 2
        for s in range(tm // IN_SUB_ROWS):
            rs = slice(s * IN_SUB_ROWS, (s + 1) * IN_SUB_ROWS)
            acc = jnp.dot(a_ref[rs, :], w_ref[slot], preferred_element_type=F32)
            for half, kind in enumerate(kinds):
                c0 = half * IN_HALF
                epilogues[kind](acc[:, c0:c0 + IN_HALF], rs, c0)

    for kinds in sorted(set(_TILE_KINDS)):
        tiles = [t for t in range(IN_TILES) if _TILE_KINDS[t] == kinds]
        lo_t, hi_t = tiles[0], tiles[-1]
        assert tiles == list(range(lo_t, hi_t + 1))
        pl.when((r - 1 >= lo_t) & (r - 1 <= hi_t))(functools.partial(run, kinds))


def _in_proj(a, w, gains, tables, seq, tm=1024):
    m, k = a.shape
    steps = m // tm
    pos_blocks = seq // tm
    last = IN_TILES - 1
    row = lambda r, i: jnp.where(r == 0, 0, i)
    tile = lambda r, i: jnp.maximum(r - 1, 0)
    tab_spec = pl.BlockSpec((tm, LANES), lambda r, i: (i % pos_blocks, 0))
    return pl.pallas_call(
        _in_proj_kernel,
        grid=(IN_TILES + 1, steps),
        in_specs=[pl.BlockSpec((tm, k), lambda r, i: (row(r, i), 0)),
                  pl.BlockSpec((k // steps, IN_TN), lambda r, i: (i, jnp.minimum(r, last))),
                  pl.BlockSpec((1, 1, IN_TN), lambda r, i: (tile(r, i), 0, 0)),
                  tab_spec, tab_spec, tab_spec, tab_spec, tab_spec],
        out_specs=pl.BlockSpec((tm, IN_TN), lambda r, i: (row(r, i), tile(r, i))),
        out_shape=jax.ShapeDtypeStruct((m, IN_COLS), BF16),
        scratch_shapes=[pltpu.VMEM((2, k, IN_TN), BF16)],
        compiler_params=_params(("arbitrary", "arbitrary")),
        name="in_proj",
    )(a, w, gains, *tables)


def _swa_kernel(bounded_ref, sink_ref, sinkrow_ref, q_ref, kp_ref, kc_ref, vp_ref, vc_ref, wf_ref,
                o_ref, wbf_ref):
    _cast_chunk(wf_ref, wbf_ref)
    i = pl.program_id(1)
    nt = (((1,), (1,)), ((), ()))
    rows = SWA_GROUP * BLOCK
    r = lax.broadcasted_iota(jnp.int32, (rows, 2 * BLOCK), 0) & (BLOCK - 1)
    c = lax.broadcasted_iota(jnp.int32, (rows, 2 * BLOCK), 1)
    valid = (c > r) & (c <= r + BLOCK) & ((i > 0) | (c >= BLOCK))
    ones = jnp.ones((2 * BLOCK, HEAD_DIM), BF16)

    def masked_scores(h):
        ks = slice(h * HEAD_DIM, (h + 1) * HEAD_DIM)
        kk = jnp.concatenate([kp_ref[:, ks], kc_ref[:, ks]], axis=0)
        vv = jnp.concatenate([vp_ref[:, ks], vc_ref[:, ks]], axis=0)
        heads = [h * SWA_GROUP + g for g in range(SWA_GROUP)]
        q4 = jnp.concatenate([q_ref[:, hd * HEAD_DIM:(hd + 1) * HEAD_DIM] for hd in heads], axis=0)
        s = lax.dot_general(q4, kk, nt, preferred_element_type=F32)
        return jnp.where(valid, s, NEG_INF), vv, heads

    @pl.when(bounded_ref[0] != 0)
    def _():
        for h in range(SWA_KV_HEADS):
            s, vv, heads = masked_scores(h)
            vaug = jnp.concatenate([vv, ones], axis=1)
            o = jnp.dot(jnp.exp2(s).astype(BF16), vaug, preferred_element_type=F32)
            for g, hd in enumerate(heads):
                rs = slice(g * BLOCK, (g + 1) * BLOCK)
                sink = jnp.exp2(sinkrow_ref[hd:hd + 1, :] * LOG2E)
                out = o[rs, :HEAD_DIM] / (o[rs, HEAD_DIM:] + sink)
                o_ref[:, hd * HEAD_DIM:(hd + 1) * HEAD_DIM] = out.astype(o_ref.dtype)

    @pl.when(bounded_ref[0] == 0)
    def _():
        for h in range(SWA_KV_HEADS):
            s, vv, heads = masked_scores(h)
            for g, hd in enumerate(heads):
                sg = s[g * BLOCK:(g + 1) * BLOCK]
                sink = sink_ref[hd] * LOG2E
                mx = jnp.maximum(jnp.max(sg, axis=-1, keepdims=True), sink)
                p = jnp.exp2(sg - mx)
                denom = jnp.sum(p, axis=-1, keepdims=True) + jnp.exp2(sink - mx)
                o = jnp.dot(p.astype(BF16), vv, preferred_element_type=F32)
                o_ref[:, hd * HEAD_DIM:(hd + 1) * HEAD_DIM] = (o / denom).astype(o_ref.dtype)


def _swa_attention(proj, bounded, sinks, w_next, batch, seq):
    nb = seq // BLOCK
    w_in_spec, w_out_spec, w_out_shape = _cast_rider(w_next, (batch, nb))
    qw = SWA_HEADS * HEAD_DIM
    kw = SWA_KV_HEADS * HEAD_DIM
    row = lambda b, i: b * nb + i
    prev = lambda b, i: b * nb + jnp.maximum(i - 1, 0)
    k_blk = SWA_Q_COLS // kw
    v_blk = k_blk + 1
    sink_rows = jnp.broadcast_to(sinks[:, None], (SWA_HEADS, HEAD_DIM))
    return pl.pallas_call(
        _swa_kernel,
        grid=(batch, nb),
        in_specs=[pl.BlockSpec(memory_space=pltpu.SMEM),
                  pl.BlockSpec(memory_space=pltpu.SMEM),
                  pl.BlockSpec((SWA_HEADS, HEAD_DIM), lambda b, i: (0, 0)),
                  pl.BlockSpec((BLOCK, qw), lambda b, i: (row(b, i), 0)),
                  pl.BlockSpec((BLOCK, kw), lambda b, i: (prev(b, i), k_blk)),
                  pl.BlockSpec((BLOCK, kw), lambda b, i: (row(b, i), k_blk)),
                  pl.BlockSpec((BLOCK, kw), lambda b, i: (prev(b, i), v_blk)),
                  pl.BlockSpec((BLOCK, kw), lambda b, i: (row(b, i), v_blk)),
                  w_in_spec],
        out_specs=[pl.BlockSpec((BLOCK, qw), lambda b, i: (row(b, i), 0)), w_out_spec],
        out_shape=[jax.ShapeDtypeStruct((batch * seq, MIX_HALF), BF16), w_out_shape],
        compiler_params=_params(("parallel", "parallel")),
        name="swa_attention",
    )(bounded, sinks, sink_rows, proj, proj, proj, proj, proj, w_next)


DIFF_TQ = 512
DIFF_TK = 256
DIFF_DIAG = DIFF_TQ // DIFF_TK
assert DIFF_DIAG * DIFF_TK == DIFF_TQ and DIFF_DIAG % 2 == 0


def _diff_kernel(bounded_ref, lq1_ref, lk1_ref, lq2_ref, lk2_ref, subg_ref,
                 q1_ref, q2_ref, k1_ref, k2_ref, v_ref, wf_ref, o_ref, wbf_ref,
                 vaug_ref, acc_ref, m_ref, l_ref, accs_ref, *, lambda_init):
    _cast_chunk(wf_ref, wbf_ref)
    qi = pl.program_id(2)
    tq = DIFF_TQ
    nt = (((1,), (1,)), ((), ()))
    lane = lax.broadcasted_iota(jnp.int32, (tq, LANES), 1)
    qs = []
    for q_ref in (q1_ref, q2_ref):
        q = q_ref[...]
        for e in range(2):
            keep = (lane >= e * DIFF_QK_DIM) & (lane < (e + 1) * DIFF_QK_DIM)
            qs.append(jnp.where(keep, q, jnp.zeros_like(q)))

    tk = DIFF_TK
    k_refs = (k1_ref, k2_ref)
    row = lax.broadcasted_iota(jnp.int32, (tq, tk), 0)
    col = lax.broadcasted_iota(jnp.int32, (tq, tk), 1)
    causal = col <= row

    def scores(slot, j, r0, masked):
        off = pl.multiple_of(j * tk, tk)
        s = lax.dot_general(qs[slot][r0:], k_refs[slot // 2][pl.ds(off, tk), :], nt,
                            preferred_element_type=F32)
        if masked:
            s = jnp.where(causal[:tq - r0], s, NEG_INF)
        return s

    lam = (jnp.exp(jnp.sum(lq1_ref[...] * lk1_ref[...], axis=-1, keepdims=True))
           - jnp.exp(jnp.sum(lq2_ref[...] * lk2_ref[...], axis=-1, keepdims=True))
           + lambda_init)

    def finalize(e, o1, o2):
        o = o1 - lam * o2
        ms = jnp.mean(o * o, axis=-1, keepdims=True)
        on = o * lax.rsqrt(ms + NORM_EPS) * subg_ref[...]
        o_ref[:, e * HEAD_DIM:(e + 1) * HEAD_DIM] = (on * (1.0 - lambda_init)).astype(o_ref.dtype)

    @pl.when(qi == 0)
    def _():
        for e in range(2):
            vaug_ref[e, :, :HEAD_DIM] = v_ref[:, e * HEAD_DIM:(e + 1) * HEAD_DIM]
            vaug_ref[e, :, HEAD_DIM:] = jnp.ones((v_ref.shape[0], HEAD_DIM), BF16)

    @pl.when(bounded_ref[0] != 0)
    def _():
        def step(j, r0, masked, assign=False):
            off = pl.multiple_of(j * tk, tk)
            ps = [jnp.exp2(scores(slot, j, r0, masked)).astype(BF16) for slot in range(4)]
            for e in range(2):
                pp = jnp.concatenate([ps[e], ps[2 + e]], axis=0)
                pv = jnp.dot(pp, vaug_ref[e, pl.ds(off, tk), :], preferred_element_type=F32)
                if assign:
                    acc_ref[e] = pv
                else:
                    acc_ref[e, r0:tq] += pv[:tq - r0]
                    acc_ref[e, tq + r0:] += pv[tq - r0:]

        step(DIFF_DIAG * qi, 0, True, assign=True)
        for d in range(1, DIFF_DIAG):
            step(DIFF_DIAG * qi + d, d * tk, True)

        def body(jj, carry):
            step(2 * jj, 0, False)
            step(2 * jj + 1, 0, False)
            return carry

        lax.fori_loop(0, (DIFF_DIAG * qi) // 2, body, 0)
        for e in range(2):
            a = acc_ref[e]
            finalize(e, a[:tq, :HEAD_DIM] / a[:tq, HEAD_DIM:],
                     a[tq:, :HEAD_DIM] / a[tq:, HEAD_DIM:])

    @pl.when(bounded_ref[0] == 0)
    def _():
        m_ref[...] = jnp.full(m_ref.shape, NEG_INF, F32)
        l_ref[...] = jnp.zeros(l_ref.shape, F32)
        accs_ref[...] = jnp.zeros(accs_ref.shape, F32)

        def step(j, r0, masked):
            off = pl.multiple_of(j * tk, tk)
            vb = v_ref[pl.ds(off, tk), :]
            for slot in range(4):
                e = slot % 2
                s = scores(slot, j, r0, masked)
                m_prev = m_ref[slot, r0:]
                m_new = jnp.maximum(m_prev, jnp.max(s, axis=-1, keepdims=True))
                alpha = jnp.exp2(m_prev - m_new)
                p = jnp.exp2(s - m_new)
                l_ref[slot, r0:] = alpha * l_ref[slot, r0:] + jnp.sum(p, axis=-1, keepdims=True)
                pv = jnp.dot(p.astype(BF16), vb[:, e * HEAD_DIM:(e + 1) * HEAD_DIM],
                             preferred_element_type=F32)
                accs_ref[slot, r0:] = alpha * accs_ref[slot, r0:] + pv
                m_ref[slot, r0:] = m_new

        def body(j, carry):
            step(j, 0, False)
            return carry

        lax.fori_loop(0, DIFF_DIAG * qi, body, 0)
        for d in range(DIFF_DIAG):
            step(DIFF_DIAG * qi + d, d * tk, True)
        for e in range(2):
            finalize(e, accs_ref[e] / l_ref[e], accs_ref[2 + e] / l_ref[2 + e])


def _diff_attention(proj, bounded, lq1, lk1, lq2, lk2, subg, w_next, batch, seq, lambda_init):
    tq = DIFF_TQ
    nq = seq // tq
    pairs = DIFF_HEADS // 2
    w_in_spec, w_out_spec, w_out_shape = _cast_rider(w_next, (batch, pairs, nq))
    map_blks = DIFF_HEADS * DIFF_QK_DIM // LANES
    vec = lambda d: pl.BlockSpec((1, d), lambda b, h, i: (0, 0))
    return pl.pallas_call(
        functools.partial(_diff_kernel, lambda_init=lambda_init),
        grid=(batch, pairs, nq),
        in_specs=[pl.BlockSpec(memory_space=pltpu.SMEM),
                  vec(DIFF_QK_DIM), vec(DIFF_QK_DIM), vec(DIFF_QK_DIM), vec(DIFF_QK_DIM),
                  vec(HEAD_DIM),
                  pl.BlockSpec((tq, LANES), lambda b, h, i: (b * nq + i, QB_BLK + h)),
                  pl.BlockSpec((tq, LANES), lambda b, h, i: (b * nq + i, QB_BLK + map_blks + h)),
                  pl.BlockSpec((seq, LANES), lambda b, h, i: (b, KB_BLK + h)),
                  pl.BlockSpec((seq, LANES), lambda b, h, i: (b, KB_BLK + map_blks + h)),
                  pl.BlockSpec((seq, 2 * HEAD_DIM), lambda b, h, i: (b, VB_BLK // 2 + h)),
                  w_in_spec],
        out_specs=[pl.BlockSpec((tq, 2 * HEAD_DIM), lambda b, h, i: (b * nq + i, h)), w_out_spec],
        out_shape=[jax.ShapeDtypeStruct((batch * seq, MIX_HALF), BF16), w_out_shape],
        scratch_shapes=[pltpu.VMEM((2, seq, 2 * HEAD_DIM), BF16),
                        pltpu.VMEM((2, 2 * tq, 2 * HEAD_DIM), F32),
                        pltpu.VMEM((4, tq, 1), F32),
                        pltpu.VMEM((4, tq, 1), F32),
                        pltpu.VMEM((4, tq, HEAD_DIM), F32)],
        compiler_params=_params(("parallel", "parallel", "arbitrary")),
        name="diff_attention",
    )(bounded, lq1.reshape(1, -1), lk1.reshape(1, -1), lq2.reshape(1, -1), lk2.reshape(1, -1),
      subg.reshape(1, -1), proj, proj, proj, proj, proj, w_next)


OUT_SUB_ROWS = 256


def _out_proj_kernel(oa_ref, ob_ref, w_ref, x_ref, g_ref, o_ref, hg_ref, ssq_ref):
    for s in range(o_ref.shape[0] // OUT_SUB_ROWS):
        rs = slice(s * OUT_SUB_ROWS, (s + 1) * OUT_SUB_ROWS)
        acc = jnp.dot(oa_ref[rs, :], w_ref[:MIX_HALF, :], preferred_element_type=F32)
        acc += jnp.dot(ob_ref[rs, :], w_ref[MIX_HALF:, :], preferred_element_type=F32)
        h = x_ref[rs, :] + acc
        o_ref[rs, :] = h
        hg_ref[rs, :] = (h * g_ref[...]).astype(hg_ref.dtype)
        ssq_ref[0, rs, :] = jnp.sum(h * h, axis=-1, keepdims=True)


def _out_proj(oa, ob, w_bf, x2d, g_next, tm=512, tn=1024):
    m, d = x2d.shape
    tile = pl.BlockSpec((tm, tn), lambda n, i: (i, n))
    return pl.pallas_call(
        _out_proj_kernel,
        grid=(d // tn, m // tm),
        in_specs=[pl.BlockSpec((tm, MIX_HALF), lambda n, i: (i, 0)),
                  pl.BlockSpec((tm, MIX_HALF), lambda n, i: (i, 0)),
                  pl.BlockSpec((2 * MIX_HALF, tn), lambda n, i: (0, n)),
                  tile,
                  pl.BlockSpec((1, tn), lambda n, i: (0, n))],
        out_specs=[tile, tile, pl.BlockSpec((1, tm, 1), lambda n, i: (n, i, 0))],
        out_shape=[jax.ShapeDtypeStruct((m, d), F32),
                   jax.ShapeDtypeStruct((m, d), BF16),
                   jax.ShapeDtypeStruct((d // tn, m, 1), F32)],
        compiler_params=_params(("parallel", "parallel")),
        name="out_proj",
    )(oa, ob, w_bf, x2d, g_next.reshape(1, d))


def _up_kernel(a_ref, ssq_ref, w_ref, wf1_ref, wf2_ref, o_ref, wbf1_ref, wbf2_ref):
    _cast_chunk(wf1_ref, wbf1_ref)
    _cast_chunk(wf2_ref, wbf2_ref)
    k = a_ref.shape[1]
    ms = jnp.sum(ssq_ref[...], axis=0) * (1.0 / k)
    inv_rms = lax.rsqrt(ms + NORM_EPS)
    acc = jnp.dot(a_ref[...], w_ref[...], preferred_element_type=F32) * inv_rms
    o_ref[...] = jnp.square(jnp.maximum(acc, 0.0)).astype(o_ref.dtype)


def _mlp_up(a, ssq, w_bf, w_next1, w_next2, tm=1024, tn=1024):
    m, k = a.shape
    n = w_bf.shape[1]
    grid = (n // tn, m // tm)
    in1, out1, shape1 = _cast_rider(w_next1, grid)
    in2, out2, shape2 = _cast_rider(w_next2, grid)
    return pl.pallas_call(
        _up_kernel,
        grid=grid,
        in_specs=[pl.BlockSpec((tm, k), lambda j, i: (i, 0)),
                  pl.BlockSpec((ssq.shape[0], tm, 1), lambda j, i: (0, i, 0)),
                  pl.BlockSpec((k, tn), lambda j, i: (0, j)),
                  in1, in2],
        out_specs=[pl.BlockSpec((tm, tn), lambda j, i: (i, j)), out1, out2],
        out_shape=[jax.ShapeDtypeStruct((m, n), BF16), shape1, shape2],
        compiler_params=_params(("parallel", "parallel")),
        name="mlp_up",
    )(a, ssq, w_bf, w_next1, w_next2)


def _down_kernel(u_ref, w_ref, h_ref, o_ref, ob_ref, acc_ref):
    kk = pl.program_id(2)

    @pl.when(kk == 0)
    def _():
        acc_ref[...] = h_ref[...]

    acc_ref[...] += jnp.dot(u_ref[...], w_ref[...], preferred_element_type=F32)

    @pl.when(kk == pl.num_programs(2) - 1)
    def _():
        h = acc_ref[...]
        o_ref[...] = h
        ob_ref[...] = h.astype(ob_ref.dtype)


def _mlp_down(u, w_bf, h2d, tm=1024, tn=1024, tk=2048):
    m, k = u.shape
    d = w_bf.shape[1]
    return pl.pallas_call(
        _down_kernel,
        grid=(d // tn, m // tm, k // tk),
        in_specs=[pl.BlockSpec((tm, tk), lambda j, i, kk: (i, kk)),
                  pl.BlockSpec((tk, tn), lambda j, i, kk: (kk, j)),
                  pl.BlockSpec((tm, tn), lambda j, i, kk: (i, j))],
        out_specs=[pl.BlockSpec((tm, tn), lambda j, i, kk: (i, j)),
                   pl.BlockSpec((tm, tn), lambda j, i, kk: (i, j))],
        out_shape=[jax.ShapeDtypeStruct((m, d), F32),
                   jax.ShapeDtypeStruct((m, d), BF16)],
        scratch_shapes=[pltpu.VMEM((tm, tn), F32)],
        compiler_params=_params(("parallel", "parallel", "arbitrary")),
        name="mlp_down",
    )(u, w_bf, h2d)


def _ple_kernel(p_ref, w_ref, g_ref, o_ref):
    y = jnp.dot(p_ref[...].astype(BF16), w_ref[...], preferred_element_type=F32)
    ms = jnp.mean(y * y, axis=-1, keepdims=True)
    o_ref[...] = y * lax.rsqrt(ms + NORM_EPS) * g_ref[...]


def _ple_embed(p2d, w_bf, g, tm=512):
    m, k = p2d.shape
    d = w_bf.shape[1]
    return pl.pallas_call(
        _ple_kernel,
        grid=(m // tm,),
        in_specs=[pl.BlockSpec((tm, k), lambda i: (i, 0)),
                  pl.BlockSpec((k, d), lambda i: (0, 0)),
                  pl.BlockSpec((1, d), lambda i: (0, 0))],
        out_specs=pl.BlockSpec((tm, d), lambda i: (i, 0)),
        out_shape=jax.ShapeDtypeStruct((m, d), F32),
        compiler_params=_params(("parallel",)),
        name="ple_embed",
    )(p2d, w_bf, g.reshape(1, d))


def _gate_kernel(hb_ref, w_ref, h_ref, pe_ref, o_ref):
    for s in range(o_ref.shape[0] // OUT_SUB_ROWS):
        rs = slice(s * OUT_SUB_ROWS, (s + 1) * OUT_SUB_ROWS)
        z = jnp.dot(hb_ref[rs, :], w_ref[...], preferred_element_type=F32)
        o_ref[rs, :] = h_ref[rs, :] + jax.nn.sigmoid(z) * pe_ref[rs, :]


def _ple_gate(hb, w_bf, h2d, pe, tm=512, tn=1024):
    m, d = h2d.shape
    tile = pl.BlockSpec((tm, tn), lambda j, i: (i, j))
    return pl.pallas_call(
        _gate_kernel,
        grid=(d // tn, m // tm),
        in_specs=[pl.BlockSpec((tm, d), lambda j, i: (i, 0)),
                  pl.BlockSpec((d, tn), lambda j, i: (0, j)),
                  tile, tile],
        out_specs=tile,
        out_shape=jax.ShapeDtypeStruct((m, d), F32),
        compiler_params=_params(("parallel", "parallel")),
        name="ple_gate",
    )(hb, w_bf, h2d, pe)


def _rope_tables(seq):
    pos = jnp.arange(seq, dtype=F32)

    def tab(dim):
        inv = 1.0 / (ROPE_THETA ** (jnp.arange(0, dim, 2, dtype=F32) / dim))
        ang = pos[:, None] * inv[None, :]
        ang = jnp.concatenate([ang, ang], axis=-1)
        return jnp.cos(ang), jnp.sin(ang)

    lane = jnp.arange(LANES)
    cos_a, sin_a = tab(HEAD_DIM)
    sin_a = jnp.where(lane < HEAD_DIM // 2, -sin_a, sin_a)
    cos_b, sin_b = tab(DIFF_QK_DIM)
    cos_b = jnp.concatenate([cos_b, cos_b], axis=-1)
    sin_b = jnp.concatenate([sin_b, sin_b], axis=-1)
    first_half = (lane % DIFF_QK_DIM) < DIFF_QK_DIM // 2
    sin_b_lo = jnp.where(first_half, -sin_b, 0.0)
    sin_b_hi = jnp.where(first_half, 0.0, sin_b)
    return cos_a, sin_a, cos_b, sin_b_lo, sin_b_hi


def _gain_rows(swa_q_g, swa_k_g, diff_q_g, diff_k_g):
    qa = jnp.tile(swa_q_g, SWA_Q_COLS // HEAD_DIM) * (LOG2E / math.sqrt(HEAD_DIM))
    ka = jnp.tile(swa_k_g, SWA_KV_COLS // HEAD_DIM)
    qb = jnp.tile(diff_q_g, DIFF_QK_COLS // DIFF_QK_DIM) * (LOG2E / math.sqrt(DIFF_QK_DIM))
    kb = jnp.tile(diff_k_g, DIFF_QK_COLS // DIFF_QK_DIM)
    cols = jnp.concatenate([qa, ka, jnp.ones((SWA_KV_COLS,), F32), qb, kb,
                            jnp.ones((DIFF_V_COLS,), F32)])
    return cols.reshape(IN_TILES, 1, IN_TN)


def _scores_bounded(q_g, k_g, dim, sinks=None):
    rounding_slack = 1.02
    bound = (math.sqrt(dim) * LOG2E * rounding_slack
             * jnp.max(jnp.abs(q_g)) * jnp.max(jnp.abs(k_g)))
    if sinks is not None:
        bound = jnp.maximum(bound, LOG2E * jnp.max(jnp.abs(sinks)))
    return (bound <= SCORE_BOUND).astype(jnp.int32).reshape(1)


def kernel(x, p, attn_norm_g, w_in, swa_q_norm_g, swa_k_norm_g, swa_sinks, diff_q_norm_g, diff_k_norm_g, diff_lambda_q1, diff_lambda_k1, diff_lambda_q2, diff_lambda_k2, diff_subln_g, w_o, mlp_norm_g, w_up, w_down, w_ple_proj, ple_norm_g, w_ple_gate):
    batch, seq, d = x.shape
    depth = w_in.shape[0]
    tables = _rope_tables(seq)
    h = x.reshape(batch * seq, d)
    for i in range(depth):
        lambda_init = 0.8 - 0.6 * math.exp(-0.3 * i)
        a = _rmsnorm_rows(h, attn_norm_g[i])
        gains = _gain_rows(swa_q_norm_g[i], swa_k_norm_g[i], diff_q_norm_g[i], diff_k_norm_g[i])
        proj = _in_proj(a, w_in[i], gains, tables, seq)
        bounded_a = _scores_bounded(swa_q_norm_g[i], swa_k_norm_g[i], HEAD_DIM, swa_sinks[i])
        out_a, w_o_bf = _swa_attention(proj, bounded_a, swa_sinks[i], w_o[i], batch, seq)
        bounded_b = _scores_bounded(diff_q_norm_g[i], diff_k_norm_g[i], DIFF_QK_DIM)
        out_b, w_up_bf = _diff_attention(proj, bounded_b, diff_lambda_q1[i], diff_lambda_k1[i],
                                         diff_lambda_q2[i], diff_lambda_k2[i], diff_subln_g[i],
                                         w_up[i], batch, seq, lambda_init)
        h, hg, ssq = _out_proj(out_a, out_b, w_o_bf, h, mlp_norm_g[i])
        u, w_down_bf, w_gate_bf = _mlp_up(hg, ssq, w_up_bf, w_down[i], w_ple_gate[i])
        h, hb = _mlp_down(u, w_down_bf, h)
        pe = _ple_embed(p[i].reshape(batch * seq, PLE_DIM), w_ple_proj[i].astype(BF16), ple_norm_g[i])
        h = _ple_gate(hb, w_gate_bf, h, pe)
    return h.reshape(batch, seq, d)
```

```python
import functools
import math

import jax
import jax.numpy as jnp
from jax import lax
from jax.experimental import pallas as pl
from jax.experimental.pallas import tpu as pltpu

F32 = jnp.float32
BF16 = jnp.bfloat16

D_MODEL = 4096
HEAD_DIM = 128
SWA_HEADS = 16
SWA_KV_HEADS = 4
SWA_GROUP = SWA_HEADS // SWA_KV_HEADS
BLOCK = 128
DIFF_HEADS = 16
DIFF_QK_DIM = 64
D_FF = 4 * D_MODEL
PLE_DIM = 256
ROPE_THETA = 10000.0
NORM_EPS = 1e-6
NEG_INF = -1e30
LOG2E = math.log2(math.e)
SCORE_BOUND = 40.0

SWA_Q_COLS = SWA_HEADS * HEAD_DIM
SWA_KV_COLS = SWA_KV_HEADS * HEAD_DIM
DIFF_QK_COLS = 2 * DIFF_HEADS * DIFF_QK_DIM
DIFF_V_COLS = DIFF_HEADS * HEAD_DIM
IN_COLS = SWA_Q_COLS + 2 * SWA_KV_COLS + 2 * DIFF_QK_COLS + DIFF_V_COLS
MIX_HALF = SWA_Q_COLS

LANES = 128
VMEM_LIMIT = 56 * 1024 * 1024

KA_BLK = SWA_Q_COLS // LANES
VA_BLK = KA_BLK + SWA_KV_COLS // LANES
QB_BLK = VA_BLK + SWA_KV_COLS // LANES
KB_BLK = QB_BLK + DIFF_QK_COLS // LANES
VB_BLK = KB_BLK + DIFF_QK_COLS // LANES

IN_TN = 1024
IN_TILES = IN_COLS // IN_TN
IN_HALF = IN_TN // 2
IN_SUB_ROWS = 256
_HALF_KINDS = (['a'] * ((SWA_Q_COLS + SWA_KV_COLS) // IN_HALF) + ['p'] * (SWA_KV_COLS // IN_HALF)
               + ['b'] * (2 * DIFF_QK_COLS // IN_HALF) + ['p'] * (DIFF_V_COLS // IN_HALF))
_TILE_KINDS = [(_HALF_KINDS[2 * t], _HALF_KINDS[2 * t + 1]) for t in range(IN_TILES)]


def _params(sem):
    return pltpu.CompilerParams(dimension_semantics=sem, vmem_limit_bytes=VMEM_LIMIT)


def _cast_rider(w, grid):
    steps = math.prod(grid)
    rows = w.shape[0] // steps
    assert rows * steps == w.shape[0] and rows % 16 == 0, (w.shape, grid)

    def index_map(*idx):
        step = idx[0]
        for size, i in zip(grid[1:], idx[1:]):
            step = step * size + i
        return (step, 0)

    spec = pl.BlockSpec((rows, w.shape[1]), index_map)
    return spec, spec, jax.ShapeDtypeStruct(w.shape, BF16)


def _cast_chunk(wf_ref, wbf_ref):
    wbf_ref[...] = wf_ref[...].astype(BF16)


def _rmsnorm_kernel(x_ref, g_ref, o_ref):
    x = x_ref[...]
    ms = jnp.mean(x * x, axis=-1, keepdims=True)
    o_ref[...] = (x * lax.rsqrt(ms + NORM_EPS) * g_ref[...]).astype(o_ref.dtype)


def _rmsnorm_rows(x2d, g, tm=512):
    m, d = x2d.shape
    return pl.pallas_call(
        _rmsnorm_kernel,
        grid=(m // tm,),
        in_specs=[pl.BlockSpec((tm, d), lambda i: (i, 0)),
                  pl.BlockSpec((1, d), lambda i: (0, 0))],
        out_specs=pl.BlockSpec((tm, d), lambda i: (i, 0)),
        out_shape=jax.ShapeDtypeStruct((m, d), BF16),
        compiler_params=_params(("parallel",)),
        name="rmsnorm_rows",
    )(x2d, g.reshape(1, d))


def _in_proj_kernel(a_ref, wf_ref, g_ref, ca_ref, sa_ref, cb_ref, sbl_ref, sbh_ref, o_ref, w_ref):
    r = pl.program_id(0)
    i = pl.program_id(1)
    tm = a_ref.shape[0]
    chunk = wf_ref.shape[0]

    @pl.when(r < IN_TILES)
    def _():
        w_ref[r % 2, pl.ds(pl.multiple_of(i * chunk, chunk), chunk), :] = wf_ref[...].astype(BF16)

    def plain(acc, rs, c0):
        o_ref[rs, c0:c0 + IN_HALF] = acc.astype(o_ref.dtype)

    def heads128(acc, rs, c0):
        for c in range(IN_HALF // LANES):
            sl = slice(c * LANES, (c + 1) * LANES)
            osl = slice(c0 + c * LANES, c0 + (c + 1) * LANES)
            y = acc[:, sl]
            ms = jnp.mean(y * y, axis=-1, keepdims=True)
            yn = y * lax.rsqrt(ms + NORM_EPS) * g_ref[0][:, osl]
            out = yn * ca_ref[rs, :] + pltpu.roll(yn, HEAD_DIM // 2, 1) * sa_ref[rs, :]
            o_ref[rs, osl] = out.astype(o_ref.dtype)

    def heads64(acc, rs, c0):
        lane = lax.broadcasted_iota(jnp.int32, (1, LANES), 1)
        lo = lane < DIFF_QK_DIM
        for c in range(IN_HALF // LANES):
            sl = slice(c * LANES, (c + 1) * LANES)
            osl = slice(c0 + c * LANES, c0 + (c + 1) * LANES)
            y = acc[:, sl]
            sq = y * y
            s_lo = jnp.sum(jnp.where(lo, sq, 0.0), axis=-1, keepdims=True)
            s_hi = jnp.sum(jnp.where(lo, 0.0, sq), axis=-1, keepdims=True)
            ms = jnp.where(lo, s_lo, s_hi) * (1.0 / DIFF_QK_DIM)
            yn = y * lax.rsqrt(ms + NORM_EPS) * g_ref[0][:, osl]
            out = (yn * cb_ref[rs, :]
                   + pltpu.roll(yn, LANES - DIFF_QK_DIM // 2, 1) * sbl_ref[rs, :]
                   + pltpu.roll(yn, DIFF_QK_DIM // 2, 1) * sbh_ref[rs, :])
            o_ref[rs, osl] = out.astype(o_ref.dtype)

    epilogues = {'a': heads128, 'b': heads64, 'p': plain}

    def run(kinds):
        slot = (r - 1) % 2
        for s in range(tm // IN_SUB_ROWS):
            rs = slice(s * IN_SUB_ROWS, (s + 1) * IN_SUB_ROWS)
            acc = jnp.dot(a_ref[rs, :], w_ref[slot], preferred_element_type=F32)
            for half, kind in enumerate(kinds):
                c0 = half * IN_HALF
                epilogues[kind](acc[:, c0:c0 + IN_HALF], rs, c0)

    for kinds in sorted(set(_TILE_KINDS)):
        tiles = [t for t in range(IN_TILES) if _TILE_KINDS[t] == kinds]
        lo_t, hi_t = tiles[0], tiles[-1]
        assert tiles == list(range(lo_t, hi_t + 1))
        pl.when((r - 1 >= lo_t) & (r - 1 <= hi_t))(functools.partial(run, kinds))


def _in_proj(a, w, gains, tables, seq, tm=1024):
    m, k = a.shape
    steps = m // tm
    pos_blocks = seq // tm
    last = IN_TILES - 1
    row = lambda r, i: jnp.where(r == 0, 0, i)
    tile = lambda r, i: jnp.maximum(r - 1, 0)
    tab_spec = pl.BlockSpec((tm, LANES), lambda r, i: (i % pos_blocks, 0))
    return pl.pallas_call(
        _in_proj_kernel,
        grid=(IN_TILES + 1, steps),
        in_specs=[pl.BlockSpec((tm, k), lambda r, i: (row(r, i), 0)),
                  pl.BlockSpec((k // steps, IN_TN), lambda r, i: (i, jnp.minimum(r, last))),
                  pl.BlockSpec((1, 1, IN_TN), lambda r, i: (tile(r, i), 0, 0)),
                  tab_spec, tab_spec, tab_spec, tab_spec, tab_spec],
        out_specs=pl.BlockSpec((tm, IN_TN), lambda r, i: (row(r, i), tile(r, i))),
        out_shape=jax.ShapeDtypeStruct((m, IN_COLS), BF16),
        scratch_shapes=[pltpu.VMEM((2, k, IN_TN), BF16)],
        compiler_params=_params(("arbitrary", "arbitrary")),
        name="in_proj",
    )(a, w, gains, *tables)


def _swa_kernel(bounded_ref, sink_ref, sinkrow_ref, q_ref, kp_ref, kc_ref, vp_ref, vc_ref, wf_ref,
                o_ref, wbf_ref):
    _cast_chunk(wf_ref, wbf_ref)
    i = pl.program_id(1)
    nt = (((1,), (1,)), ((), ()))
    rows = SWA_GROUP * BLOCK
    r = lax.broadcasted_iota(jnp.int32, (rows, 2 * BLOCK), 0) & (BLOCK - 1)
    c = lax.broadcasted_iota(jnp.int32, (rows, 2 * BLOCK), 1)
    valid = (c > r) & (c <= r + BLOCK) & ((i > 0) | (c >= BLOCK))
    ones = jnp.ones((2 * BLOCK, HEAD_DIM), BF16)

    def masked_scores(h):
        ks = slice(h * HEAD_DIM, (h + 1) * HEAD_DIM)
        kk = jnp.concatenate([kp_ref[:, ks], kc_ref[:, ks]], axis=0)
        vv = jnp.concatenate([vp_ref[:, ks], vc_ref[:, ks]], axis=0)
        heads = [h * SWA_GROUP + g for g in range(SWA_GROUP)]
        q4 = jnp.concatenate([q_ref[:, hd * HEAD_DIM:(hd + 1) * HEAD_DIM] for hd in heads], axis=0)
        s = lax.dot_general(q4, kk, nt, preferred_element_type=F32)
        return jnp.where(valid, s, NEG_INF), vv, heads

    @pl.when(bounded_ref[0] != 0)
    def _():
        for h in range(SWA_KV_HEADS):
            s, vv, heads = masked_scores(h)
            vaug = jnp.concatenate([vv, ones], axis=1)
            o = jnp.dot(jnp.exp2(s).astype(BF16), vaug, preferred_element_type=F32)
            for g, hd in enumerate(heads):
                rs = slice(g * BLOCK, (g + 1) * BLOCK)
                sink = jnp.exp2(sinkrow_ref[hd:hd + 1, :] * LOG2E)
                out = o[rs, :HEAD_DIM] / (o[rs, HEAD_DIM:] + sink)
                o_ref[:, hd * HEAD_DIM:(hd + 1) * HEAD_DIM] = out.astype(o_ref.dtype)

    @pl.when(bounded_ref[0] == 0)
    def _():
        for h in range(SWA_KV_HEADS):
            s, vv, heads = masked_scores(h)
            for g, hd in enumerate(heads):
                sg = s[g * BLOCK:(g + 1) * BLOCK]
                sink = sink_ref[hd] * LOG2E
                mx = jnp.maximum(jnp.max(sg, axis=-1, keepdims=True), sink)
                p = jnp.exp2(sg - mx)
                denom = jnp.sum(p, axis=-1, keepdims=True) + jnp.exp2(sink - mx)
                o = jnp.dot(p.astype(BF16), vv, preferred_element_type=F32)
                o_ref[:, hd * HEAD_DIM:(hd + 1) * HEAD_DIM] = (o / denom).astype(o_ref.dtype)


def _swa_attention(proj, bounded, sinks, w_next, batch, seq):
    nb = seq // BLOCK
    w_in_spec, w_out_spec, w_out_shape = _cast_rider(w_next, (batch, nb))
    qw = SWA_HEADS * HEAD_DIM
    kw = SWA_KV_HEADS * HEAD_DIM
    row = lambda b, i: b * nb + i
    prev = lambda b, i: b * nb + jnp.maximum(i - 1, 0)
    k_blk = SWA_Q_COLS // kw
    v_blk = k_blk + 1
    sink_rows = jnp.broadcast_to(sinks[:, None], (SWA_HEADS, HEAD_DIM))
    return pl.pallas_call(
        _swa_kernel,
        grid=(batch, nb),
        in_specs=[pl.BlockSpec(memory_space=pltpu.SMEM),
                  pl.BlockSpec(memory_space=pltpu.SMEM),
                  pl.BlockSpec((SWA_HEADS, HEAD_DIM), lambda b, i: (0, 0)),
                  pl.BlockSpec((BLOCK, qw), lambda b, i: (row(b, i), 0)),
                  pl.BlockSpec((BLOCK, kw), lambda b, i: (prev(b, i), k_blk)),
                  pl.BlockSpec((BLOCK, kw), lambda b, i: (row(b, i), k_blk)),
                  pl.BlockSpec((BLOCK, kw), lambda b, i: (prev(b, i), v_blk)),
                  pl.BlockSpec((BLOCK, kw), lambda b, i: (row(b, i), v_blk)),
                  w_in_spec],
        out_specs=[pl.BlockSpec((BLOCK, qw), lambda b, i: (row(b, i), 0)), w_out_spec],
        out_shape=[jax.ShapeDtypeStruct((batch * seq, MIX_HALF), BF16), w_out_shape],
        compiler_params=_params(("parallel", "parallel")),
        name="swa_attention",
    )(bounded, sinks, sink_rows, proj, proj, proj, proj, proj, w_next)


DIFF_TQ = 512
DIFF_TK = 256
DIFF_DIAG = DIFF_TQ // DIFF_TK
assert DIFF_DIAG * DIFF_TK == DIFF_TQ and DIFF_DIAG % 2 == 0


def _diff_kernel(bounded_ref, lq1_ref, lk1_ref, lq2_ref, lk2_ref, subg_ref,
                 q1_ref, q2_ref, k1_ref, k2_ref, v_ref, wf_ref, o_ref, wbf_ref,
                 vaug_ref, acc_ref, m_ref, l_ref, accs_ref, *, lambda_init):
    _cast_chunk(wf_ref, wbf_ref)
    qi = pl.program_id(2)
    tq = DIFF_TQ
    nt = (((1,), (1,)), ((), ()))
    lane = lax.broadcasted_iota(jnp.int32, (tq, LANES), 1)
    qs = []
    for q_ref in (q1_ref, q2_ref):
        q = q_ref[...]
        for e in range(2):
            keep = (lane >= e * DIFF_QK_DIM) & (lane < (e + 1) * DIFF_QK_DIM)
            qs.append(jnp.where(keep, q, jnp.zeros_like(q)))

    tk = DIFF_TK
    k_refs = (k1_ref, k2_ref)
    row = lax.broadcasted_iota(jnp.int32, (tq, tk), 0)
    col = lax.broadcasted_iota(jnp.int32, (tq, tk), 1)
    causal = col <= row

    def scores(slot, j, r0, masked):
        off = pl.multiple_of(j * tk, tk)
        s = lax.dot_general(qs[slot][r0:], k_refs[slot // 2][pl.ds(off, tk), :], nt,
                            preferred_element_type=F32)
        if masked:
            s = jnp.where(causal[:tq - r0], s, NEG_INF)
        return s

    lam = (jnp.exp(jnp.sum(lq1_ref[...] * lk1_ref[...], axis=-1, keepdims=True))
           - jnp.exp(jnp.sum(lq2_ref[...] * lk2_ref[...], axis=-1, keepdims=True))
           + lambda_init)

    def finalize(e, o1, o2):
        o = o1 - lam * o2
        ms = jnp.mean(o * o, axis=-1, keepdims=True)
        on = o * lax.rsqrt(ms + NORM_EPS) * subg_ref[...]
        o_ref[:, e * HEAD_DIM:(e + 1) * HEAD_DIM] = (on * (1.0 - lambda_init)).astype(o_ref.dtype)

    @pl.when(qi == 0)
    def _():
        for e in range(2):
            vaug_ref[e, :, :HEAD_DIM] = v_ref[:, e * HEAD_DIM:(e + 1) * HEAD_DIM]
            vaug_ref[e, :, HEAD_DIM:] = jnp.ones((v_ref.shape[0], HEAD_DIM), BF16)

    @pl.when(bounded_ref[0] != 0)
    def _():
        def step(j, r0, masked, assign=False):
            off = pl.multiple_of(j * tk, tk)
            ps = [jnp.exp2(scores(slot, j, r0, masked)).astype(BF16) for slot in range(4)]
            for e in range(2):
                pp = jnp.concatenate([ps[e], ps[2 + e]], axis=0)
                pv = jnp.dot(pp, vaug_ref[e, pl.ds(off, tk), :], preferred_element_type=F32)
                if assign:
                    acc_ref[e] = pv
                else:
                    acc_ref[e, r0:tq] += pv[:tq - r0]
                    acc_ref[e, tq + r0:] += pv[tq - r0:]

        step(DIFF_DIAG * qi, 0, True, assign=True)
        for d in range(1, DIFF_DIAG):
            step(DIFF_DIAG * qi + d, d * tk, True)

        def body(jj, carry):
            step(2 * jj, 0, False)
            step(2 * jj + 1, 0, False)
            return carry

        lax.fori_loop(0, (DIFF_DIAG * qi) // 2, body, 0)
        for e in range(2):
            a = acc_ref[e]
            finalize(e, a[:tq, :HEAD_DIM] / a[:tq, HEAD_DIM:],
                     a[tq:, :HEAD_DIM] / a[tq:, HEAD_DIM:])

    @pl.when(bounded_ref[0] == 0)
    def _():
        m_ref[...] = jnp.full(m_ref.shape, NEG_INF, F32)
        l_ref[...] = jnp.zeros(l_ref.shape, F32)
        accs_ref[...] = jnp.zeros(accs_ref.shape, F32)

        def step(j, r0, masked):
            off = pl.multiple_of(j * tk, tk)
            vb = v_ref[pl.ds(off, tk), :]
            for slot in range(4):
                e = slot % 2
                s = scores(slot, j, r0, masked)
                m_prev = m_ref[slot, r0:]
                m_new = jnp.maximum(m_prev, jnp.max(s, axis=-1, keepdims=True))
                alpha = jnp.exp2(m_prev - m_new)
                p = jnp.exp2(s - m_new)
                l_ref[slot, r0:] = alpha * l_ref[slot, r0:] + jnp.sum(p, axis=-1, keepdims=True)
                pv = jnp.dot(p.astype(BF16), vb[:, e * HEAD_DIM:(e + 1) * HEAD_DIM],
                             preferred_element_type=F32)
                accs_ref[slot, r0:] = alpha * accs_ref[slot, r0:] + pv
                m_ref[slot, r0:] = m_new

        def body(j, carry):
            step(j, 0, False)
            return carry

        lax.fori_loop(0, DIFF_DIAG * qi, body, 0)
        for d in range(DIFF_DIAG):
            step(DIFF_DIAG * qi + d, d * tk, True)
        for e in range(2):
            finalize(e, accs_ref[e] / l_ref[e], accs_ref[2 + e] / l_ref[2 + e])


def _diff_attention(proj, bounded, lq1, lk1, lq2, lk2, subg, w_next, batch, seq, lambda_init):
    tq = DIFF_TQ
    nq = seq // tq
    pairs = DIFF_HEADS // 2
    w_in_spec, w_out_spec, w_out_shape = _cast_rider(w_next, (batch, pairs, nq))
    map_blks = DIFF_HEADS * DIFF_QK_DIM // LANES
    vec = lambda d: pl.BlockSpec((1, d), lambda b, h, i: (0, 0))
    return pl.pallas_call(
        functools.partial(_diff_kernel, lambda_init=lambda_init),
        grid=(batch, pairs, nq),
        in_specs=[pl.BlockSpec(memory_space=pltpu.SMEM),
                  vec(DIFF_QK_DIM), vec(DIFF_QK_DIM), vec(DIFF_QK_DIM), vec(DIFF_QK_DIM),
                  vec(HEAD_DIM),
                  pl.BlockSpec((tq, LANES), lambda b, h, i: (b * nq + i, QB_BLK + h)),
                  pl.BlockSpec((tq, LANES), lambda b, h, i: (b * nq + i, QB_BLK + map_blks + h)),
                  pl.BlockSpec((seq, LANES), lambda b, h, i: (b, KB_BLK + h)),
                  pl.BlockSpec((seq, LANES), lambda b, h, i: (b, KB_BLK + map_blks + h)),
                  pl.BlockSpec((seq, 2 * HEAD_DIM), lambda b, h, i: (b, VB_BLK // 2 + h)),
                  w_in_spec],
        out_specs=[pl.BlockSpec((tq, 2 * HEAD_DIM), lambda b, h, i: (b * nq + i, h)), w_out_spec],
        out_shape=[jax.ShapeDtypeStruct((batch * seq, MIX_HALF), BF16), w_out_shape],
        scratch_shapes=[pltpu.VMEM((2, seq, 2 * HEAD_DIM), BF16),
                        pltpu.VMEM((2, 2 * tq, 2 * HEAD_DIM), F32),
                        pltpu.VMEM((4, tq, 1), F32),
                        pltpu.VMEM((4, tq, 1), F32),
                        pltpu.VMEM((4, tq, HEAD_DIM), F32)],
        compiler_params=_params(("parallel", "parallel", "arbitrary")),
        name="diff_attention",
    )(bounded, lq1.reshape(1, -1), lk1.reshape(1, -1), lq2.reshape(1, -1), lk2.reshape(1, -1),
      subg.reshape(1, -1), proj, proj, proj, proj, proj, w_next)


OUT_SUB_ROWS = 256


def _out_proj_kernel(oa_ref, ob_ref, w_ref, x_ref, g_ref, o_ref, hg_ref, ssq_ref):
    for s in range(o_ref.shape[0] // OUT_SUB_ROWS):
        rs = slice(s * OUT_SUB_ROWS, (s + 1) * OUT_SUB_ROWS)
        acc = jnp.dot(oa_ref[rs, :], w_ref[:MIX_HALF, :], preferred_element_type=F32)
        acc += jnp.dot(ob_ref[rs, :], w_ref[MIX_HALF:, :], preferred_element_type=F32)
        h = x_ref[rs, :] + acc
        o_ref[rs, :] = h
        hg_ref[rs, :] = (h * g_ref[...]).astype(hg_ref.dtype)
        ssq_ref[0, rs, :] = jnp.sum(h * h, axis=-1, keepdims=True)


def _out_proj(oa, ob, w_bf, x2d, g_next, tm=512, tn=1024):
    m, d = x2d.shape
    tile = pl.BlockSpec((tm, tn), lambda n, i: (i, n))
    return pl.pallas_call(
        _out_proj_kernel,
        grid=(d // tn, m // tm),
        in_specs=[pl.BlockSpec((tm, MIX_HALF), lambda n, i: (i, 0)),
                  pl.BlockSpec((tm, MIX_HALF), lambda n, i: (i, 0)),
                  pl.BlockSpec((2 * MIX_HALF, tn), lambda n, i: (0, n)),
                  tile,
                  pl.BlockSpec((1, tn), lambda n, i: (0, n))],
        out_specs=[tile, tile, pl.BlockSpec((1, tm, 1), lambda n, i: (n, i, 0))],
        out_shape=[jax.ShapeDtypeStruct((m, d), F32),
                   jax.ShapeDtypeStruct((m, d), BF16),
                   jax.ShapeDtypeStruct((d // tn, m, 1), F32)],
        compiler_params=_params(("parallel", "parallel")),
        name="out_proj",
    )(oa, ob, w_bf, x2d, g_next.reshape(1, d))


def _up_kernel(a_ref, ssq_ref, w_ref, wf1_ref, wf2_ref, o_ref, wbf1_ref, wbf2_ref):
    _cast_chunk(wf1_ref, wbf1_ref)
    _cast_chunk(wf2_ref, wbf2_ref)
    k = a_ref.shape[1]
    ms = jnp.sum(ssq_ref[...], axis=0) * (1.0 / k)
    inv_rms = lax.rsqrt(ms + NORM_EPS)
    acc = jnp.dot(a_ref[...], w_ref[...], preferred_element_type=F32) * inv_rms
    o_ref[...] = jnp.square(jnp.maximum(acc, 0.0)).astype(o_ref.dtype)


def _mlp_up(a, ssq, w_bf, w_next1, w_next2, tm=1024, tn=1024):
    m, k = a.shape
    n = w_bf.shape[1]
    grid = (n // tn, m // tm)
    in1, out1, shape1 = _cast_rider(w_next1, grid)
    in2, out2, shape2 = _cast_rider(w_next2, grid)
    return pl.pallas_call(
        _up_kernel,
        grid=grid,
        in_specs=[pl.BlockSpec((tm, k), lambda j, i: (i, 0)),
                  pl.BlockSpec((ssq.shape[0], tm, 1), lambda j, i: (0, i, 0)),
                  pl.BlockSpec((k, tn), lambda j, i: (0, j)),
                  in1, in2],
        out_specs=[pl.BlockSpec((tm, tn), lambda j, i: (i, j)), out1, out2],
        out_shape=[jax.ShapeDtypeStruct((m, n), BF16), shape1, shape2],
        compiler_params=_params(("parallel", "parallel")),
        name="mlp_up",
    )(a, ssq, w_bf, w_next1, w_next2)


def _down_kernel(u_ref, w_ref, h_ref, o_ref, ob_ref, acc_ref):
    kk = pl.program_id(2)

    @pl.when(kk == 0)
    def _():
        acc_ref[...] = h_ref[...]

    acc_ref[...] += jnp.dot(u_ref[...], w_ref[...], preferred_element_type=F32)

    @pl.when(kk == pl.num_programs(2) - 1)
    def _():
        h = acc_ref[...]
        o_ref[...] = h
        ob_ref[...] = h.astype(ob_ref.dtype)


def _mlp_down(u, w_bf, h2d, tm=1024, tn=1024, tk=2048):
    m, k = u.shape
    d = w_bf.shape[1]
    return pl.pallas_call(
        _down_kernel,
        grid=(d // tn, m // tm, k // tk),
        in_specs=[pl.BlockSpec((tm, tk), lambda j, i, kk: (i, kk)),
                  pl.BlockSpec((tk, tn), lambda j, i, kk: (kk, j)),
                  pl.BlockSpec((tm, tn), lambda j, i, kk: (i, j))],
        out_specs=[pl.BlockSpec((tm, tn), lambda j, i, kk: (i, j)),
                   pl.BlockSpec((tm, tn), lambda j, i, kk: (i, j))],
        out_shape=[jax.ShapeDtypeStruct((m, d), F32),
                   jax.ShapeDtypeStruct((m, d), BF16)],
        scratch_shapes=[pltpu.VMEM((tm, tn), F32)],
        compiler_params=_params(("parallel", "parallel", "arbitrary")),
        name="mlp_down",
    )(u, w_bf, h2d)


def _ple_kernel(p_ref, w_ref, g_ref, o_ref):
    y = jnp.dot(p_ref[...].astype(BF16), w_ref[...], preferred_element_type=F32)
    ms = jnp.mean(y * y, axis=-1, keepdims=True)
    o_ref[...] = y * lax.rsqrt(ms + NORM_EPS) * g_ref[...]


def _ple_embed(p2d, w_bf, g, tm=512):
    m, k = p2d.shape
    d = w_bf.shape[1]
    return pl.pallas_call(
        _ple_kernel,
        grid=(m // tm,),
        in_specs=[pl.BlockSpec((tm, k), lambda i: (i, 0)),
                  pl.BlockSpec((k, d), lambda i: (0, 0)),
                  pl.BlockSpec((1, d), lambda i: (0, 0))],
        out_specs=pl.BlockSpec((tm, d), lambda i: (i, 0)),
        out_shape=jax.ShapeDtypeStruct((m, d), F32),
        compiler_params=_params(("parallel",)),
        name="ple_embed",
    )(p2d, w_bf, g.reshape(1, d))


def _gate_kernel(hb_ref, w_ref, h_ref, pe_ref, o_ref):
    for s in range(o_ref.shape[0] // OUT_SUB_ROWS):
        rs = slice(s * OUT_SUB_ROWS, (s + 1) * OUT_SUB_ROWS)
        z = jnp.dot(hb_ref[rs, :], w_ref[...], preferred_element_type=F32)
        o_ref[rs, :] = h_ref[rs, :] + jax.nn.sigmoid(z) * pe_ref[rs, :]


def _ple_gate(hb, w_bf, h2d, pe, tm=512, tn=1024):
    m, d = h2d.shape
    tile = pl.BlockSpec((tm, tn), lambda j, i: (i, j))
    return pl.pallas_call(
        _gate_kernel,
        grid=(d // tn, m // tm),
        in_specs=[pl.BlockSpec((tm, d), lambda j, i: (i, 0)),
                  pl.BlockSpec((d, tn), lambda j, i: (0, j)),
                  tile, tile],
        out_specs=tile,
        out_shape=jax.ShapeDtypeStruct((m, d), F32),
        compiler_params=_params(("parallel", "parallel")),
        name="ple_gate",
    )(hb, w_bf, h2d, pe)


def _rope_tables(seq):
    pos = jnp.arange(seq, dtype=F32)

    def tab(dim):
        inv = 1.0 / (ROPE_THETA ** (jnp.arange(0, dim, 2, dtype=F32) / dim))
        ang = pos[:, None] * inv[None, :]
        ang = jnp.concatenate([ang, ang], axis=-1)
        return jnp.cos(ang), jnp.sin(ang)

    lane = jnp.arange(LANES)
    cos_a, sin_a = tab(HEAD_DIM)
    sin_a = jnp.where(lane < HEAD_DIM // 2, -sin_a, sin_a)
    cos_b, sin_b = tab(DIFF_QK_DIM)
    cos_b = jnp.concatenate([cos_b, cos_b], axis=-1)
    sin_b = jnp.concatenate([sin_b, sin_b], axis=-1)
    first_half = (lane % DIFF_QK_DIM) < DIFF_QK_DIM // 2
    sin_b_lo = jnp.where(first_half, -sin_b, 0.0)
    sin_b_hi = jnp.where(first_half, 0.0, sin_b)
    return cos_a, sin_a, cos_b, sin_b_lo, sin_b_hi


def _gain_rows(swa_q_g, swa_k_g, diff_q_g, diff_k_g):
    qa = jnp.tile(swa_q_g, SWA_Q_COLS // HEAD_DIM) * (LOG2E / math.sqrt(HEAD_DIM))
    ka = jnp.tile(swa_k_g, SWA_KV_COLS // HEAD_DIM)
    qb = jnp.tile(diff_q_g, DIFF_QK_COLS // DIFF_QK_DIM) * (LOG2E / math.sqrt(DIFF_QK_DIM))
    kb = jnp.tile(diff_k_g, DIFF_QK_COLS // DIFF_QK_DIM)
    cols = jnp.concatenate([qa, ka, jnp.ones((SWA_KV_COLS,), F32), qb, kb,
                            jnp.ones((DIFF_V_COLS,), F32)])
    return cols.reshape(IN_TILES, 1, IN_TN)


def _scores_bounded(q_g, k_g, dim, sinks=None):
    rounding_slack = 1.02
    bound = (math.sqrt(dim) * LOG2E * rounding_slack
             * jnp.max(jnp.abs(q_g)) * jnp.max(jnp.abs(k_g)))
    if sinks is not None:
        bound = jnp.maximum(bound, LOG2E * jnp.max(jnp.abs(sinks)))
    return (bound <= SCORE_BOUND).astype(jnp.int32).reshape(1)


def kernel(x, p, attn_norm_g, w_in, swa_q_norm_g, swa_k_norm_g, swa_sinks, diff_q_norm_g, diff_k_norm_g, diff_lambda_q1, diff_lambda_k1, diff_lambda_q2, diff_lambda_k2, diff_subln_g, w_o, mlp_norm_g, w_up, w_down, w_ple_proj, ple_norm_g, w_ple_gate):
    batch, seq, d = x.shape
    depth = w_in.shape[0]
    tables = _rope_tables(seq)
    h = x.reshape(batch * seq, d)
    for i in range(depth):
        lambda_init = 0.8 - 0.6 * math.exp(-0.3 * i)
        a = _rmsnorm_rows(h, attn_norm_g[i])
        gains = _gain_rows(swa_q_norm_g[i], swa_k_norm_g[i], diff_q_norm_g[i], diff_k_norm_g[i])
        proj = _in_proj(a, w_in[i], gains, tables, seq)
        bounded_a = _scores_bounded(swa_q_norm_g[i], swa_k_norm_g[i], HEAD_DIM, swa_sinks[i])
        out_a, w_o_bf = _swa_attention(proj, bounded_a, swa_sinks[i], w_o[i], batch, seq)
        bounded_b = _scores_bounded(diff_q_norm_g[i], diff_k_norm_g[i], DIFF_QK_DIM)
        out_b, w_up_bf = _diff_attention(proj, bounded_b, diff_lambda_q1[i], diff_lambda_k1[i],
                                         diff_lambda_q2[i], diff_lambda_k2[i], diff_subln_g[i],
                                         w_up[i], batch, seq, lambda_init)
        h, hg, ssq = _out_proj(out_a, out_b, w_o_bf, h, mlp_norm_g[i])
        u, w_down_bf, w_gate_bf = _mlp_up(hg, ssq, w_up_bf, w_down[i], w_ple_gate[i])
        h, hb = _mlp_down(u, w_down_bf, h)
        pe = _ple_embed(p[i].reshape(batch * seq, PLE_DIM), w_ple_proj[i].astype(BF16), ple_norm_g[i])
        h = _ple_gate(hb, w_gate_bf, h, pe)
    return h.reshape(batch, seq, d)
```

```python
import functools
import math

import jax
import jax.numpy as jnp
from jax import lax
from jax.experimental import pallas as pl
from jax.experimental.pallas import tpu as pltpu

F32 = jnp.float32
BF16 = jnp.bfloat16

D_MODEL = 4096
HEAD_DIM = 128
SWA_HEADS = 16
SWA_KV_HEADS = 4
SWA_GROUP = SWA_HEADS // SWA_KV_HEADS
BLOCK = 128
DIFF_HEADS = 16
DIFF_QK_DIM = 64
D_FF = 4 * D_MODEL
PLE_DIM = 256
ROPE_THETA = 10000.0
NORM_EPS = 1e-6
NEG_INF = -1e30
LOG2E = math.log2(math.e)
SCORE_BOUND = 40.0

SWA_Q_COLS = SWA_HEADS * HEAD_DIM
SWA_KV_COLS = SWA_KV_HEADS * HEAD_DIM
DIFF_QK_COLS = 2 * DIFF_HEADS * DIFF_QK_DIM
DIFF_V_COLS = DIFF_HEADS * HEAD_DIM
IN_COLS = SWA_Q_COLS + 2 * SWA_KV_COLS + 2 * DIFF_QK_COLS + DIFF_V_COLS
MIX_HALF = SWA_Q_COLS

LANES = 128
VMEM_LIMIT = 56 * 1024 * 1024

KA_BLK = SWA_Q_COLS // LANES
VA_BLK = KA_BLK + SWA_KV_COLS // LANES
QB_BLK = VA_BLK + SWA_KV_COLS // LANES
KB_BLK = QB_BLK + DIFF_QK_COLS // LANES
VB_BLK = KB_BLK + DIFF_QK_COLS // LANES

IN_TN = 1024
IN_TILES = IN_COLS // IN_TN
IN_HALF = IN_TN // 2
IN_SUB_ROWS = 256
IN_RIDER_ROUNDS = IN_TILES - 1
_HALF_KINDS = (['a'] * ((SWA_Q_COLS + SWA_KV_COLS) // IN_HALF) + ['p'] * (SWA_KV_COLS // IN_HALF)
               + ['b'] * (2 * DIFF_QK_COLS // IN_HALF) + ['p'] * (DIFF_V_COLS // IN_HALF))
_TILE_KINDS = [(_HALF_KINDS[2 * t], _HALF_KINDS[2 * t + 1]) for t in range(IN_TILES)]


def _params(sem):
    return pltpu.CompilerParams(dimension_semantics=sem, vmem_limit_bytes=VMEM_LIMIT)


def _cast_rider(w, grid):
    steps = math.prod(grid)
    rows = w.shape[0] // steps
    assert rows * steps == w.shape[0] and rows % 16 == 0, (w.shape, grid)

    def index_map(*idx):
        step = idx[0]
        for size, i in zip(grid[1:], idx[1:]):
            step = step * size + i
        return (step, 0)

    spec = pl.BlockSpec((rows, w.shape[1]), index_map)
    return spec, spec, jax.ShapeDtypeStruct(w.shape, BF16)


def _cast_chunk(wf_ref, wbf_ref):
    wbf_ref[...] = wf_ref[...].astype(BF16)


def _rmsnorm_kernel(x_ref, g_ref, o_ref):
    x = x_ref[...]
    ms = jnp.mean(x * x, axis=-1, keepdims=True)
    o_ref[...] = (x * lax.rsqrt(ms + NORM_EPS) * g_ref[...]).astype(o_ref.dtype)


def _rmsnorm_rows(x2d, g, tm=512):
    m, d = x2d.shape
    return pl.pallas_call(
        _rmsnorm_kernel,
        grid=(m // tm,),
        in_specs=[pl.BlockSpec((tm, d), lambda i: (i, 0)),
                  pl.BlockSpec((1, d), lambda i: (0, 0))],
        out_specs=pl.BlockSpec((tm, d), lambda i: (i, 0)),
        out_shape=jax.ShapeDtypeStruct((m, d), BF16),
        compiler_params=_params(("parallel",)),
        name="rmsnorm_rows",
    )(x2d, g.reshape(1, d))


def _in_proj_kernel(a_ref, wf_ref, g_ref, ca_ref, sa_ref, cb_ref, sbl_ref, sbh_ref, nextf_ref,
                    o_ref, nextbf_ref, w_ref):
    r = pl.program_id(0)
    i = pl.program_id(1)
    tm = a_ref.shape[0]
    chunk = wf_ref.shape[0]

    @pl.when((r >= 1) & (r <= IN_RIDER_ROUNDS))
    def _():
        _cast_chunk(nextf_ref, nextbf_ref)

    @pl.when(r < IN_TILES)
    def _():
        w_ref[r % 2, pl.ds(pl.multiple_of(i * chunk, chunk), chunk), :] = wf_ref[...].astype(BF16)

    def plain(acc, rs, c0):
        o_ref[rs, c0:c0 + IN_HALF] = acc.astype(o_ref.dtype)

    def heads128(acc, rs, c0):
        for c in range(IN_HALF // LANES):
            sl = slice(c * LANES, (c + 1) * LANES)
            osl = slice(c0 + c * LANES, c0 + (c + 1) * LANES)
            y = acc[:, sl]
            ms = jnp.mean(y * y, axis=-1, keepdims=True)
            yn = y * lax.rsqrt(ms + NORM_EPS) * g_ref[0][:, osl]
            out = yn * ca_ref[rs, :] + pltpu.roll(yn, HEAD_DIM // 2, 1) * sa_ref[rs, :]
            o_ref[rs, osl] = out.astype(o_ref.dtype)

    def heads64(acc, rs, c0):
        lane = lax.broadcasted_iota(jnp.int32, (1, LANES), 1)
        lo = lane < DIFF_QK_DIM
        for c in range(IN_HALF // LANES):
            sl = slice(c * LANES, (c + 1) * LANES)
            osl = slice(c0 + c * LANES, c0 + (c + 1) * LANES)
            y = acc[:, sl]
            sq = y * y
            s_lo = jnp.sum(jnp.where(lo, sq, 0.0), axis=-1, keepdims=True)
            s_hi = jnp.sum(jnp.where(lo, 0.0, sq), axis=-1, keepdims=True)
            ms = jnp.where(lo, s_lo, s_hi) * (1.0 / DIFF_QK_DIM)
            yn = y * lax.rsqrt(ms + NORM_EPS) * g_ref[0][:, osl]
            out = (yn * cb_ref[rs, :]
                   + pltpu.roll(yn, LANES - DIFF_QK_DIM // 2, 1) * sbl_ref[rs, :]
                   + pltpu.roll(yn, DIFF_QK_DIM // 2, 1) * sbh_ref[rs, :])
            o_ref[rs, osl] = out.astype(o_ref.dtype)

    epilogues = {'a': heads128, 'b': heads64, 'p': plain}

    def run(kinds):
        slot = (r - 1) % 2
        for s in range(tm // IN_SUB_ROWS):
            rs = slice(s * IN_SUB_ROWS, (s + 1) * IN_SUB_ROWS)
            acc = jnp.dot(a_ref[rs, :], w_ref[slot], preferred_element_type=F32)
            for half, kind in enumerate(kinds):
                c0 = half * IN_HALF
                epilogues[kind](acc[:, c0:c0 + IN_HALF], rs, c0)

    for kinds in sorted(set(_TILE_KINDS)):
        tiles = [t for t in range(IN_TILES) if _TILE_KINDS[t] == kinds]
        lo_t, hi_t = tiles[0], tiles[-1]
        assert tiles == list(range(lo_t, hi_t + 1))
        pl.when((r - 1 >= lo_t) & (r - 1 <= hi_t))(functools.partial(run, kinds))


def _in_proj(a, w, gains, tables, w_next, seq, tm=1024):
    m, k = a.shape
    steps = m // tm
    pos_blocks = seq // tm
    last = IN_TILES - 1
    rider_chunks = IN_RIDER_ROUNDS * steps
    rider_rows = w_next.shape[0] // rider_chunks
    assert rider_rows * rider_chunks == w_next.shape[0] and rider_rows % 16 == 0
    row = lambda r, i: jnp.where(r == 0, 0, i)
    tile = lambda r, i: jnp.maximum(r - 1, 0)
    tab_spec = pl.BlockSpec((tm, LANES), lambda r, i: (i % pos_blocks, 0))
    rider_spec = pl.BlockSpec((rider_rows, w_next.shape[1]),
                              lambda r, i: (jnp.clip((r - 1) * steps + i, 0, rider_chunks - 1), 0))
    return pl.pallas_call(
        _in_proj_kernel,
        grid=(IN_TILES + 1, steps),
        in_specs=[pl.BlockSpec((tm, k), lambda r, i: (row(r, i), 0)),
                  pl.BlockSpec((k // steps, IN_TN), lambda r, i: (i, jnp.minimum(r, last))),
                  pl.BlockSpec((1, 1, IN_TN), lambda r, i: (tile(r, i), 0, 0)),
                  tab_spec, tab_spec, tab_spec, tab_spec, tab_spec, rider_spec],
        out_specs=[pl.BlockSpec((tm, IN_TN), lambda r, i: (row(r, i), tile(r, i))), rider_spec],
        out_shape=[jax.ShapeDtypeStruct((m, IN_COLS), BF16),
                   jax.ShapeDtypeStruct(w_next.shape, BF16)],
        scratch_shapes=[pltpu.VMEM((2, k, IN_TN), BF16)],
        compiler_params=_params(("arbitrary", "arbitrary")),
        name="in_proj",
    )(a, w, gains, *tables, w_next)


SWA_STEP_BLOCKS = 2


def _swa_kernel(bounded_ref, sink_ref, sinkrow_ref, q_ref, kp_ref, kc_ref, vp_ref, vc_ref, o_ref):
    j = pl.program_id(1)
    nt = (((1,), (1,)), ((), ()))
    rows = SWA_GROUP * BLOCK
    r = lax.broadcasted_iota(jnp.int32, (rows, 2 * BLOCK), 0) & (BLOCK - 1)
    c = lax.broadcasted_iota(jnp.int32, (rows, 2 * BLOCK), 1)
    window = (c > r) & (c <= r + BLOCK)
    valid = [window & ((j > 0) | (c >= BLOCK))] + [window] * (SWA_STEP_BLOCKS - 1)
    ones = jnp.ones((2 * BLOCK, HEAD_DIM), BF16)

    def masked_scores(h, t):
        ks = slice(h * HEAD_DIM, (h + 1) * HEAD_DIM)
        qr = slice(t * BLOCK, (t + 1) * BLOCK)
        if t == 0:
            kk = jnp.concatenate([kp_ref[:, ks], kc_ref[:BLOCK, ks]], axis=0)
            vv = jnp.concatenate([vp_ref[:, ks], vc_ref[:BLOCK, ks]], axis=0)
        else:
            kk = kc_ref[(t - 1) * BLOCK:(t + 1) * BLOCK, ks]
            vv = vc_ref[(t - 1) * BLOCK:(t + 1) * BLOCK, ks]
        heads = [h * SWA_GROUP + g for g in range(SWA_GROUP)]
        q4 = jnp.concatenate([q_ref[qr, hd * HEAD_DIM:(hd + 1) * HEAD_DIM] for hd in heads], axis=0)
        s = lax.dot_general(q4, kk, nt, preferred_element_type=F32)
        return jnp.where(valid[t], s, NEG_INF), vv, heads, qr

    @pl.when(bounded_ref[0] != 0)
    def _():
        for h in range(SWA_KV_HEADS):
            for t in range(SWA_STEP_BLOCKS):
                s, vv, heads, qr = masked_scores(h, t)
                vaug = jnp.concatenate([vv, ones], axis=1)
                o = jnp.dot(jnp.exp2(s).astype(BF16), vaug, preferred_element_type=F32)
                for g, hd in enumerate(heads):
                    rs = slice(g * BLOCK, (g + 1) * BLOCK)
                    sink = jnp.exp2(sinkrow_ref[hd:hd + 1, :] * LOG2E)
                    out = o[rs, :HEAD_DIM] / (o[rs, HEAD_DIM:] + sink)
                    o_ref[qr, hd * HEAD_DIM:(hd + 1) * HEAD_DIM] = out.astype(o_ref.dtype)

    @pl.when(bounded_ref[0] == 0)
    def _():
        for h in range(SWA_KV_HEADS):
            for t in range(SWA_STEP_BLOCKS):
                s, vv, heads, qr = masked_scores(h, t)
                for g, hd in enumerate(heads):
                    sg = s[g * BLOCK:(g + 1) * BLOCK]
                    sink = sink_ref[hd] * LOG2E
                    mx = jnp.maximum(jnp.max(sg, axis=-1, keepdims=True), sink)
                    p = jnp.exp2(sg - mx)
                    denom = jnp.sum(p, axis=-1, keepdims=True) + jnp.exp2(sink - mx)
                    o = jnp.dot(p.astype(BF16), vv, preferred_element_type=F32)
                    o_ref[qr, hd * HEAD_DIM:(hd + 1) * HEAD_DIM] = (o / denom).astype(o_ref.dtype)


def _swa_attention(proj, bounded, sinks, batch, seq):
    step_rows = SWA_STEP_BLOCKS * BLOCK
    ns = seq // step_rows
    qw = SWA_HEADS * HEAD_DIM
    kw = SWA_KV_HEADS * HEAD_DIM
    row = lambda b, j: b * ns + j
    prev = lambda b, j: (b * ns + j) * SWA_STEP_BLOCKS - jnp.minimum(j, 1)
    k_blk = SWA_Q_COLS // kw
    v_blk = k_blk + 1
    sink_rows = jnp.broadcast_to(sinks[:, None], (SWA_HEADS, HEAD_DIM))
    return pl.pallas_call(
        _swa_kernel,
        grid=(batch, ns),
        in_specs=[pl.BlockSpec(memory_space=pltpu.SMEM),
                  pl.BlockSpec(memory_space=pltpu.SMEM),
                  pl.BlockSpec((SWA_HEADS, HEAD_DIM), lambda b, j: (0, 0)),
                  pl.BlockSpec((step_rows, qw), lambda b, j: (row(b, j), 0)),
                  pl.BlockSpec((BLOCK, kw), lambda b, j: (prev(b, j), k_blk)),
                  pl.BlockSpec((step_rows, kw), lambda b, j: (row(b, j), k_blk)),
                  pl.BlockSpec((BLOCK, kw), lambda b, j: (prev(b, j), v_blk)),
                  pl.BlockSpec((step_rows, kw), lambda b, j: (row(b, j), v_blk))],
        out_specs=pl.BlockSpec((step_rows, qw), lambda b, j: (row(b, j), 0)),
        out_shape=jax.ShapeDtypeStruct((batch * seq, MIX_HALF), BF16),
        compiler_params=_params(("parallel", "parallel")),
        name="swa_attention",
    )(bounded, sinks, sink_rows, proj, proj, proj, proj, proj)


DIFF_TQ = 512
DIFF_TK = 256
DIFF_DIAG = DIFF_TQ // DIFF_TK
assert DIFF_DIAG * DIFF_TK == DIFF_TQ and DIFF_DIAG % 2 == 0


def _diff_kernel(bounded_ref, lq1_ref, lk1_ref, lq2_ref, lk2_ref, subg_ref,
                 q1_ref, q2_ref, k1_ref, k2_ref, v_ref, wf_ref, o_ref, wbf_ref,
                 vaug_ref, acc_ref, m_ref, l_ref, accs_ref, *, lambda_init):
    _cast_chunk(wf_ref, wbf_ref)
    qi = pl.program_id(2)
    tq = DIFF_TQ
    nt = (((1,), (1,)), ((), ()))
    lane = lax.broadcasted_iota(jnp.int32, (tq, LANES), 1)
    qs = []
    for q_ref in (q1_ref, q2_ref):
        q = q_ref[...]
        for e in range(2):
            keep = (lane >= e * DIFF_QK_DIM) & (lane < (e + 1) * DIFF_QK_DIM)
            qs.append(jnp.where(keep, q, jnp.zeros_like(q)))

    tk = DIFF_TK
    k_refs = (k1_ref, k2_ref)
    row = lax.broadcasted_iota(jnp.int32, (tq, tk), 0)
    col = lax.broadcasted_iota(jnp.int32, (tq, tk), 1)
    causal = col <= row

    def scores(slot, j, r0, masked):
        off = pl.multiple_of(j * tk, tk)
        s = lax.dot_general(qs[slot][r0:], k_refs[slot // 2][pl.ds(off, tk), :], nt,
                            preferred_element_type=F32)
        if masked:
            s = jnp.where(causal[:tq - r0], s, NEG_INF)
        return s

    lam = (jnp.exp(jnp.sum(lq1_ref[...] * lk1_ref[...], axis=-1, keepdims=True))
           - jnp.exp(jnp.sum(lq2_ref[...] * lk2_ref[...], axis=-1, keepdims=True))
           + lambda_init)

    def finalize(e, o1, o2):
        o = o1 - lam * o2
        ms = jnp.mean(o * o, axis=-1, keepdims=True)
        on = o * lax.rsqrt(ms + NORM_EPS) * subg_ref[...]
        o_ref[:, e * HEAD_DIM:(e + 1) * HEAD_DIM] = (on * (1.0 - lambda_init)).astype(o_ref.dtype)

    @pl.when(qi == 0)
    def _():
        for e in range(2):
            vaug_ref[e, :, :HEAD_DIM] = v_ref[:, e * HEAD_DIM:(e + 1) * HEAD_DIM]
            vaug_ref[e, :, HEAD_DIM:] = jnp.ones((v_ref.shape[0], HEAD_DIM), BF16)

    @pl.when(bounded_ref[0] != 0)
    def _():
        def step(j, r0, masked, assign=False):
            off = pl.multiple_of(j * tk, tk)
            ps = [jnp.exp2(scores(slot, j, r0, masked)).astype(BF16) for slot in range(4)]
            for e in range(2):
                pp = jnp.concatenate([ps[e], ps[2 + e]], axis=0)
                pv = jnp.dot(pp, vaug_ref[e, pl.ds(off, tk), :], preferred_element_type=F32)
                if assign:
                    acc_ref[e] = pv
                else:
                    acc_ref[e, r0:tq] += pv[:tq - r0]
                    acc_ref[e, tq + r0:] += pv[tq - r0:]

        step(DIFF_DIAG * qi, 0, True, assign=True)
        for d in range(1, DIFF_DIAG):
            step(DIFF_DIAG * qi + d, d * tk, True)

        def body(jj, carry):
            step(2 * jj, 0, False)
            step(2 * jj + 1, 0, False)
            return carry

        lax.fori_loop(0, (DIFF_DIAG * qi) // 2, body, 0)
        for e in range(2):
            a = acc_ref[e]
            finalize(e, a[:tq, :HEAD_DIM] / a[:tq, HEAD_DIM:],
                     a[tq:, :HEAD_DIM] / a[tq:, HEAD_DIM:])

    @pl.when(bounded_ref[0] == 0)
    def _():
        m_ref[...] = jnp.full(m_ref.shape, NEG_INF, F32)
        l_ref[...] = jnp.zeros(l_ref.shape, F32)
        accs_ref[...] = jnp.zeros(accs_ref.shape, F32)

        def step(j, r0, masked):
            off = pl.multiple_of(j * tk, tk)
            vb = v_ref[pl.ds(off, tk), :]
            for slot in range(4):
                e = slot % 2
                s = scores(slot, j, r0, masked)
                m_prev = m_ref[slot, r0:]
                m_new = jnp.maximum(m_prev, jnp.max(s, axis=-1, keepdims=True))
                alpha = jnp.exp2(m_prev - m_new)
                p = jnp.exp2(s - m_new)
                l_ref[slot, r0:] = alpha * l_ref[slot, r0:] + jnp.sum(p, axis=-1, keepdims=True)
                pv = jnp.dot(p.astype(BF16), vb[:, e * HEAD_DIM:(e + 1) * HEAD_DIM],
                             preferred_element_type=F32)
                accs_ref[slot, r0:] = alpha * accs_ref[slot, r0:] + pv
                m_ref[slot, r0:] = m_new

        def body(j, carry):
            step(j, 0, False)
            return carry

        lax.fori_loop(0, DIFF_DIAG * qi, body, 0)
        for d in range(DIFF_DIAG):
            step(DIFF_DIAG * qi + d, d * tk, True)
        for e in range(2):
            finalize(e, accs_ref[e] / l_ref[e], accs_ref[2 + e] / l_ref[2 + e])


def _diff_attention(proj, bounded, lq1, lk1, lq2, lk2, subg, w_next, batch, seq, lambda_init):
    tq = DIFF_TQ
    nq = seq // tq
    pairs = DIFF_HEADS // 2
    w_in_spec, w_out_spec, w_out_shape = _cast_rider(w_next, (batch, pairs, nq))
    map_blks = DIFF_HEADS * DIFF_QK_DIM // LANES
    vec = lambda d: pl.BlockSpec((1, d), lambda b, h, i: (0, 0))
    return pl.pallas_call(
        functools.partial(_diff_kernel, lambda_init=lambda_init),
        grid=(batch, pairs, nq),
        in_specs=[pl.BlockSpec(memory_space=pltpu.SMEM),
                  vec(DIFF_QK_DIM), vec(DIFF_QK_DIM), vec(DIFF_QK_DIM), vec(DIFF_QK_DIM),
                  vec(HEAD_DIM),
                  pl.BlockSpec((tq, LANES), lambda b, h, i: (b * nq + i, QB_BLK + h)),
                  pl.BlockSpec((tq, LANES), lambda b, h, i: (b * nq + i, QB_BLK + map_blks + h)),
                  pl.BlockSpec((seq, LANES), lambda b, h, i: (b, KB_BLK + h)),
                  pl.BlockSpec((seq, LANES), lambda b, h, i: (b, KB_BLK + map_blks + h)),
                  pl.BlockSpec((seq, 2 * HEAD_DIM), lambda b, h, i: (b, VB_BLK // 2 + h)),
                  w_in_spec],
        out_specs=[pl.BlockSpec((tq, 2 * HEAD_DIM), lambda b, h, i: (b * nq + i, h)), w_out_spec],
        out_shape=[jax.ShapeDtypeStruct((batch * seq, MIX_HALF), BF16), w_out_shape],
        scratch_shapes=[pltpu.VMEM((2, seq, 2 * HEAD_DIM), BF16),
                        pltpu.VMEM((2, 2 * tq, 2 * HEAD_DIM), F32),
                        pltpu.VMEM((4, tq, 1), F32),
                        pltpu.VMEM((4, tq, 1), F32),
                        pltpu.VMEM((4, tq, HEAD_DIM), F32)],
        compiler_params=_params(("parallel", "parallel", "arbitrary")),
        name="diff_attention",
    )(bounded, lq1.reshape(1, -1), lk1.reshape(1, -1), lq2.reshape(1, -1), lk2.reshape(1, -1),
      subg.reshape(1, -1), proj, proj, proj, proj, proj, w_next)


OUT_SUB_ROWS = 512


def _out_proj_kernel(oa_ref, ob_ref, w_ref, x_ref, g_ref, o_ref, hg_ref, ssq_ref):
    for s in range(o_ref.shape[0] // OUT_SUB_ROWS):
        rs = slice(s * OUT_SUB_ROWS, (s + 1) * OUT_SUB_ROWS)
        acc = jnp.dot(oa_ref[rs, :], w_ref[:MIX_HALF, :], preferred_element_type=F32)
        acc += jnp.dot(ob_ref[rs, :], w_ref[MIX_HALF:, :], preferred_element_type=F32)
        h = x_ref[rs, :] + acc
        o_ref[rs, :] = h
        hg_ref[rs, :] = (h * g_ref[...]).astype(hg_ref.dtype)
        ssq_ref[0, rs, :] = jnp.sum(h * h, axis=-1, keepdims=True)


def _out_proj(oa, ob, w_bf, x2d, g_next, tm=1024, tn=1024):
    m, d = x2d.shape
    tile = pl.BlockSpec((tm, tn), lambda n, i: (i, n))
    return pl.pallas_call(
        _out_proj_kernel,
        grid=(d // tn, m // tm),
        in_specs=[pl.BlockSpec((tm, MIX_HALF), lambda n, i: (i, 0)),
                  pl.BlockSpec((tm, MIX_HALF), lambda n, i: (i, 0)),
                  pl.BlockSpec((2 * MIX_HALF, tn), lambda n, i: (0, n)),
                  tile,
                  pl.BlockSpec((1, tn), lambda n, i: (0, n))],
        out_specs=[tile, tile, pl.BlockSpec((1, tm, 1), lambda n, i: (n, i, 0))],
        out_shape=[jax.ShapeDtypeStruct((m, d), F32),
                   jax.ShapeDtypeStruct((m, d), BF16),
                   jax.ShapeDtypeStruct((d // tn, m, 1), F32)],
        compiler_params=_params(("parallel", "parallel")),
        name="out_proj",
    )(oa, ob, w_bf, x2d, g_next.reshape(1, d))


def _up_kernel(a_ref, ssq_ref, w_ref, wf1_ref, wf2_ref, o_ref, wbf1_ref, wbf2_ref):
    _cast_chunk(wf1_ref, wbf1_ref)
    _cast_chunk(wf2_ref, wbf2_ref)
    k = a_ref.shape[1]
    ms = jnp.sum(ssq_ref[...], axis=0) * (1.0 / k)
    inv_rms = lax.rsqrt(ms + NORM_EPS)
    acc = jnp.dot(a_ref[...], w_ref[...], preferred_element_type=F32) * inv_rms
    o_ref[...] = jnp.square(jnp.maximum(acc, 0.0)).astype(o_ref.dtype)


def _mlp_up(a, ssq, w_bf, w_next1, w_next2, tm=1024, tn=1024):
    m, k = a.shape
    n = w_bf.shape[1]
    grid = (n // tn, m // tm)
    in1, out1, shape1 = _cast_rider(w_next1, grid)
    in2, out2, shape2 = _cast_rider(w_next2, grid)
    return pl.pallas_call(
        _up_kernel,
        grid=grid,
        in_specs=[pl.BlockSpec((tm, k), lambda j, i: (i, 0)),
                  pl.BlockSpec((ssq.shape[0], tm, 1), lambda j, i: (0, i, 0)),
                  pl.BlockSpec((k, tn), lambda j, i: (0, j)),
                  in1, in2],
        out_specs=[pl.BlockSpec((tm, tn), lambda j, i: (i, j)), out1, out2],
        out_shape=[jax.ShapeDtypeStruct((m, n), BF16), shape1, shape2],
        compiler_params=_params(("parallel", "parallel")),
        name="mlp_up",
    )(a, ssq, w_bf, w_next1, w_next2)


DOWN_SUB_ROWS = 512


def _down_kernel(u_ref, w_ref, h_ref, o_ref, ob_ref):
    kk = pl.program_id(2)

    def accumulate(first):
        for s in range(o_ref.shape[0] // DOWN_SUB_ROWS):
            rs = slice(s * DOWN_SUB_ROWS, (s + 1) * DOWN_SUB_ROWS)
            part = jnp.dot(u_ref[rs, :], w_ref[...], preferred_element_type=F32)
            o_ref[rs, :] = (h_ref[rs, :] if first else o_ref[rs, :]) + part

    pl.when(kk == 0)(lambda: accumulate(True))
    pl.when(kk > 0)(lambda: accumulate(False))

    @pl.when(kk == pl.num_programs(2) - 1)
    def _():
        ob_ref[...] = o_ref[...].astype(ob_ref.dtype)


def _mlp_down(u, w_bf, h2d, tm=1024, tn=1024, tk=4096):
    m, k = u.shape
    d = w_bf.shape[1]
    return pl.pallas_call(
        _down_kernel,
        grid=(d // tn, m // tm, k // tk),
        in_specs=[pl.BlockSpec((tm, tk), lambda j, i, kk: (i, kk)),
                  pl.BlockSpec((tk, tn), lambda j, i, kk: (kk, j)),
                  pl.BlockSpec((tm, tn), lambda j, i, kk: (i, j))],
        out_specs=[pl.BlockSpec((tm, tn), lambda j, i, kk: (i, j)),
                   pl.BlockSpec((tm, tn), lambda j, i, kk: (i, j))],
        out_shape=[jax.ShapeDtypeStruct((m, d), F32),
                   jax.ShapeDtypeStruct((m, d), BF16)],
        compiler_params=_params(("parallel", "parallel", "arbitrary")),
        name="mlp_down",
    )(u, w_bf, h2d)


def _ple_kernel(p_ref, w_ref, g_ref, o_ref):
    y = jnp.dot(p_ref[...].astype(BF16), w_ref[...], preferred_element_type=F32)
    ms = jnp.mean(y * y, axis=-1, keepdims=True)
    o_ref[...] = y * lax.rsqrt(ms + NORM_EPS) * g_ref[...]


def _ple_embed(p2d, w_bf, g, tm=512):
    m, k = p2d.shape
    d = w_bf.shape[1]
    return pl.pallas_call(
        _ple_kernel,
        grid=(m // tm,),
        in_specs=[pl.BlockSpec((tm, k), lambda i: (i, 0)),
                  pl.BlockSpec((k, d), lambda i: (0, 0)),
                  pl.BlockSpec((1, d), lambda i: (0, 0))],
        out_specs=pl.BlockSpec((tm, d), lambda i: (i, 0)),
        out_shape=jax.ShapeDtypeStruct((m, d), F32),
        compiler_params=_params(("parallel",)),
        name="ple_embed",
    )(p2d, w_bf, g.reshape(1, d))


def _gate_kernel(hb_ref, w_ref, h_ref, pe_ref, o_ref):
    for s in range(o_ref.shape[0] // OUT_SUB_ROWS):
        rs = slice(s * OUT_SUB_ROWS, (s + 1) * OUT_SUB_ROWS)
        z = jnp.dot(hb_ref[rs, :], w_ref[...], preferred_element_type=F32)
        o_ref[rs, :] = h_ref[rs, :] + jax.nn.sigmoid(z) * pe_ref[rs, :]


def _ple_gate(hb, w_bf, h2d, pe, tm=512, tn=1024):
    m, d = h2d.shape
    tile = pl.BlockSpec((tm, tn), lambda j, i: (i, j))
    return pl.pallas_call(
        _gate_kernel,
        grid=(d // tn, m // tm),
        in_specs=[pl.BlockSpec((tm, d), lambda j, i: (i, 0)),
                  pl.BlockSpec((d, tn), lambda j, i: (0, j)),
                  tile, tile],
        out_specs=tile,
        out_shape=jax.ShapeDtypeStruct((m, d), F32),
        compiler_params=_params(("parallel", "parallel")),
        name="ple_gate",
    )(hb, w_bf, h2d, pe)


def _rope_tables(seq):
    pos = jnp.arange(seq, dtype=F32)

    def tab(dim):
        inv = 1.0 / (ROPE_THETA ** (jnp.arange(0, dim, 2, dtype=F32) / dim))
        ang = pos[:, None] * inv[None, :]
        ang = jnp.concatenate([ang, ang], axis=-1)
        return jnp.cos(ang), jnp.sin(ang)

    lane = jnp.arange(LANES)
    cos_a, sin_a = tab(HEAD_DIM)
    sin_a = jnp.where(lane < HEAD_DIM // 2, -sin_a, sin_a)
    cos_b, sin_b = tab(DIFF_QK_DIM)
    cos_b = jnp.concatenate([cos_b, cos_b], axis=-1)
    sin_b = jnp.concatenate([sin_b, sin_b], axis=-1)
    first_half = (lane % DIFF_QK_DIM) < DIFF_QK_DIM // 2
    sin_b_lo = jnp.where(first_half, -sin_b, 0.0)
    sin_b_hi = jnp.where(first_half, 0.0, sin_b)
    return cos_a, sin_a, cos_b, sin_b_lo, sin_b_hi


def _gain_rows(swa_q_g, swa_k_g, diff_q_g, diff_k_g):
    qa = jnp.tile(swa_q_g, SWA_Q_COLS // HEAD_DIM) * (LOG2E / math.sqrt(HEAD_DIM))
    ka = jnp.tile(swa_k_g, SWA_KV_COLS // HEAD_DIM)
    qb = jnp.tile(diff_q_g, DIFF_QK_COLS // DIFF_QK_DIM) * (LOG2E / math.sqrt(DIFF_QK_DIM))
    kb = jnp.tile(diff_k_g, DIFF_QK_COLS // DIFF_QK_DIM)
    cols = jnp.concatenate([qa, ka, jnp.ones((SWA_KV_COLS,), F32), qb, kb,
                            jnp.ones((DIFF_V_COLS,), F32)])
    return cols.reshape(IN_TILES, 1, IN_TN)


def _scores_bounded(q_g, k_g, dim, sinks=None):
    rounding_slack = 1.02
    bound = (math.sqrt(dim) * LOG2E * rounding_slack
             * jnp.max(jnp.abs(q_g)) * jnp.max(jnp.abs(k_g)))
    if sinks is not None:
        bound = jnp.maximum(bound, LOG2E * jnp.max(jnp.abs(sinks)))
    return (bound <= SCORE_BOUND).astype(jnp.int32).reshape(1)


def kernel(x, p, attn_norm_g, w_in, swa_q_norm_g, swa_k_norm_g, swa_sinks, diff_q_norm_g, diff_k_norm_g, diff_lambda_q1, diff_lambda_k1, diff_lambda_q2, diff_lambda_k2, diff_subln_g, w_o, mlp_norm_g, w_up, w_down, w_ple_proj, ple_norm_g, w_ple_gate):
    batch, seq, d = x.shape
    depth = w_in.shape[0]
    tables = _rope_tables(seq)
    h = x.reshape(batch * seq, d)
    for i in range(depth):
        lambda_init = 0.8 - 0.6 * math.exp(-0.3 * i)
        a = _rmsnorm_rows(h, attn_norm_g[i])
        gains = _gain_rows(swa_q_norm_g[i], swa_k_norm_g[i], diff_q_norm_g[i], diff_k_norm_g[i])
        proj, w_o_bf = _in_proj(a, w_in[i], gains, tables, w_o[i], seq)
        bounded_a = _scores_bounded(swa_q_norm_g[i], swa_k_norm_g[i], HEAD_DIM, swa_sinks[i])
        out_a = _swa_attention(proj, bounded_a, swa_sinks[i], batch, seq)
        bounded_b = _scores_bounded(diff_q_norm_g[i], diff_k_norm_g[i], DIFF_QK_DIM)
        out_b, w_up_bf = _diff_attention(proj, bounded_b, diff_lambda_q1[i], diff_lambda_k1[i],
                                         diff_lambda_q2[i], diff_lambda_k2[i], diff_subln_g[i],
                                         w_up[i], batch, seq, lambda_init)
        h, hg, ssq = _out_proj(out_a, out_b, w_o_bf, h, mlp_norm_g[i])
        u, w_down_bf, w_gate_bf = _mlp_up(hg, ssq, w_up_bf, w_down[i], w_ple_gate[i])
        h, hb = _mlp_down(u, w_down_bf, h)
        pe = _ple_embed(p[i].reshape(batch * seq, PLE_DIM), w_ple_proj[i].astype(BF16), ple_norm_g[i])
        h = _ple_gate(hb, w_gate_bf, h, pe)
    return h.reshape(batch, seq, d)
```

```python
import functools
import math

import jax
import jax.numpy as jnp
from jax import lax
from jax.experimental import pallas as pl
from jax.experimental.pallas import tpu as pltpu

F32 = jnp.float32
BF16 = jnp.bfloat16

D_MODEL = 4096
HEAD_DIM = 128
SWA_HEADS = 16
SWA_KV_HEADS = 4
SWA_GROUP = SWA_HEADS // SWA_KV_HEADS
BLOCK = 128
DIFF_HEADS = 16
DIFF_QK_DIM = 64
D_FF = 4 * D_MODEL
PLE_DIM = 256
ROPE_THETA = 10000.0
NORM_EPS = 1e-6
NEG_INF = -1e30
LOG2E = math.log2(math.e)
SCORE_BOUND = 40.0

SWA_Q_COLS = SWA_HEADS * HEAD_DIM
SWA_KV_COLS = SWA_KV_HEADS * HEAD_DIM
DIFF_QK_COLS = 2 * DIFF_HEADS * DIFF_QK_DIM
DIFF_V_COLS = DIFF_HEADS * HEAD_DIM
IN_COLS = SWA_Q_COLS + 2 * SWA_KV_COLS + 2 * DIFF_QK_COLS + DIFF_V_COLS
MIX_HALF = SWA_Q_COLS

LANES = 128
VMEM_LIMIT = 56 * 1024 * 1024

KA_BLK = SWA_Q_COLS // LANES
VA_BLK = KA_BLK + SWA_KV_COLS // LANES
QB_BLK = VA_BLK + SWA_KV_COLS // LANES
KB_BLK = QB_BLK + DIFF_QK_COLS // LANES
VB_BLK = KB_BLK + DIFF_QK_COLS // LANES

IN_TN = 1024
IN_TILES = IN_COLS // IN_TN
IN_HALF = IN_TN // 2
IN_SUB_ROWS = 256
IN_RIDER_ROUNDS = IN_TILES - 1
_HALF_KINDS = (['a'] * ((SWA_Q_COLS + SWA_KV_COLS) // IN_HALF) + ['p'] * (SWA_KV_COLS // IN_HALF)
               + ['b'] * (2 * DIFF_QK_COLS // IN_HALF) + ['p'] * (DIFF_V_COLS // IN_HALF))
_TILE_KINDS = [(_HALF_KINDS[2 * t], _HALF_KINDS[2 * t + 1]) for t in range(IN_TILES)]


def _params(sem):
    return pltpu.CompilerParams(dimension_semantics=sem, vmem_limit_bytes=VMEM_LIMIT)


def _cast_rider(w, grid):
    steps = math.prod(grid)
    rows = w.shape[0] // steps
    assert rows * steps == w.shape[0] and rows % 16 == 0, (w.shape, grid)

    def index_map(*idx):
        step = idx[0]
        for size, i in zip(grid[1:], idx[1:]):
            step = step * size + i
        return (step, 0)

    spec = pl.BlockSpec((rows, w.shape[1]), index_map)
    return spec, spec, jax.ShapeDtypeStruct(w.shape, BF16)


def _cast_chunk(wf_ref, wbf_ref):
    wbf_ref[...] = wf_ref[...].astype(BF16)


def _rmsnorm_kernel(x_ref, g_ref, o_ref):
    x = x_ref[...]
    ms = jnp.mean(x * x, axis=-1, keepdims=True)
    o_ref[...] = (x * lax.rsqrt(ms + NORM_EPS) * g_ref[...]).astype(o_ref.dtype)


def _rmsnorm_rows(x2d, g, tm=512):
    m, d = x2d.shape
    return pl.pallas_call(
        _rmsnorm_kernel,
        grid=(m // tm,),
        in_specs=[pl.BlockSpec((tm, d), lambda i: (i, 0)),
                  pl.BlockSpec((1, d), lambda i: (0, 0))],
        out_specs=pl.BlockSpec((tm, d), lambda i: (i, 0)),
        out_shape=jax.ShapeDtypeStruct((m, d), BF16),
        compiler_params=_params(("parallel",)),
        name="rmsnorm_rows",
    )(x2d, g.reshape(1, d))


def _in_proj_kernel(a_ref, wf_ref, g_ref, ca_ref, sa_ref, cb_ref, sbl_ref, sbh_ref, nextf_ref,
                    o_ref, nextbf_ref, w_ref):
    r = pl.program_id(0)
    i = pl.program_id(1)
    tm = a_ref.shape[0]
    chunk = wf_ref.shape[0]

    @pl.when((r >= 1) & (r <= IN_RIDER_ROUNDS))
    def _():
        _cast_chunk(nextf_ref, nextbf_ref)

    @pl.when(r < IN_TILES)
    def _():
        w_ref[r % 2, pl.ds(pl.multiple_of(i * chunk, chunk), chunk), :] = wf_ref[...].astype(BF16)

    def plain(acc, rs, c0):
        o_ref[rs, c0:c0 + IN_HALF] = acc.astype(o_ref.dtype)

    def heads128(acc, rs, c0):
        for c in range(IN_HALF // LANES):
            sl = slice(c * LANES, (c + 1) * LANES)
            osl = slice(c0 + c * LANES, c0 + (c + 1) * LANES)
            y = acc[:, sl]
            ms = jnp.mean(y * y, axis=-1, keepdims=True)
            yn = y * lax.rsqrt(ms + NORM_EPS) * g_ref[0][:, osl]
            out = yn * ca_ref[rs, :] + pltpu.roll(yn, HEAD_DIM // 2, 1) * sa_ref[rs, :]
            o_ref[rs, osl] = out.astype(o_ref.dtype)

    def heads64(acc, rs, c0):
        lane = lax.broadcasted_iota(jnp.int32, (1, LANES), 1)
        lo = lane < DIFF_QK_DIM
        for c in range(IN_HALF // LANES):
            sl = slice(c * LANES, (c + 1) * LANES)
            osl = slice(c0 + c * LANES, c0 + (c + 1) * LANES)
            y = acc[:, sl]
            sq = y * y
            s_lo = jnp.sum(jnp.where(lo, sq, 0.0), axis=-1, keepdims=True)
            s_hi = jnp.sum(jnp.where(lo, 0.0, sq), axis=-1, keepdims=True)
            ms = jnp.where(lo, s_lo, s_hi) * (1.0 / DIFF_QK_DIM)
            yn = y * lax.rsqrt(ms + NORM_EPS) * g_ref[0][:, osl]
            out = (yn * cb_ref[rs, :]
                   + pltpu.roll(yn, LANES - DIFF_QK_DIM // 2, 1) * sbl_ref[rs, :]
                   + pltpu.roll(yn, DIFF_QK_DIM // 2, 1) * sbh_ref[rs, :])
            o_ref[rs, osl] = out.astype(o_ref.dtype)

    epilogues = {'a': heads128, 'b': heads64, 'p': plain}

    def run(kinds):
        slot = (r - 1) % 2
        for s in range(tm // IN_SUB_ROWS):
            rs = slice(s * IN_SUB_ROWS, (s + 1) * IN_SUB_ROWS)
            acc = jnp.dot(a_ref[rs, :], w_ref[slot], preferred_element_type=F32)
            for half, kind in enumerate(kinds):
                c0 = half * IN_HALF
                epilogues[kind](acc[:, c0:c0 + IN_HALF], rs, c0)

    for kinds in sorted(set(_TILE_KINDS)):
        tiles = [t for t in range(IN_TILES) if _TILE_KINDS[t] == kinds]
        lo_t, hi_t = tiles[0], tiles[-1]
        assert tiles == list(range(lo_t, hi_t + 1))
        pl.when((r - 1 >= lo_t) & (r - 1 <= hi_t))(functools.partial(run, kinds))


def _in_proj(a, w, gains, tables, w_next, seq, tm=1024):
    m, k = a.shape
    steps = m // tm
    pos_blocks = seq // tm
    last = IN_TILES - 1
    rider_chunks = IN_RIDER_ROUNDS * steps
    rider_rows = w_next.shape[0] // rider_chunks
    assert rider_rows * rider_chunks == w_next.shape[0] and rider_rows % 16 == 0
    row = lambda r, i: jnp.where(r == 0, 0, i)
    tile = lambda r, i: jnp.maximum(r - 1, 0)
    tab_spec = pl.BlockSpec((tm, LANES), lambda r, i: (i % pos_blocks, 0))
    rider_spec = pl.BlockSpec((rider_rows, w_next.shape[1]),
                              lambda r, i: (jnp.clip((r - 1) * steps + i, 0, rider_chunks - 1), 0))
    return pl.pallas_call(
        _in_proj_kernel,
        grid=(IN_TILES + 1, steps),
        in_specs=[pl.BlockSpec((tm, k), lambda r, i: (row(r, i), 0)),
                  pl.BlockSpec((k // steps, IN_TN), lambda r, i: (i, jnp.minimum(r, last))),
                  pl.BlockSpec((1, 1, IN_TN), lambda r, i: (tile(r, i), 0, 0)),
                  tab_spec, tab_spec, tab_spec, tab_spec, tab_spec, rider_spec],
        out_specs=[pl.BlockSpec((tm, IN_TN), lambda r, i: (row(r, i), tile(r, i))), rider_spec],
        out_shape=[jax.ShapeDtypeStruct((m, IN_COLS), BF16),
                   jax.ShapeDtypeStruct(w_next.shape, BF16)],
        scratch_shapes=[pltpu.VMEM((2, k, IN_TN), BF16)],
        compiler_params=_params(("arbitrary", "arbitrary")),
        name="in_proj",
    )(a, w, gains, *tables, w_next)


SWA_STEP_BLOCKS = 2


def _swa_kernel(bounded_ref, sink_ref, sinkrow_ref, q_ref, kp_ref, kc_ref, vp_ref, vc_ref, o_ref):
    j = pl.program_id(1)
    nt = (((1,), (1,)), ((), ()))
    rows = SWA_GROUP * BLOCK
    r = lax.broadcasted_iota(jnp.int32, (rows, 2 * BLOCK), 0) & (BLOCK - 1)
    c = lax.broadcasted_iota(jnp.int32, (rows, 2 * BLOCK), 1)
    window = (c > r) & (c <= r + BLOCK)
    valid = [window & ((j > 0) | (c >= BLOCK))] + [window] * (SWA_STEP_BLOCKS - 1)
    ones = jnp.ones((2 * BLOCK, HEAD_DIM), BF16)

    def masked_scores(h, t):
        ks = slice(h * HEAD_DIM, (h + 1) * HEAD_DIM)
        qr = slice(t * BLOCK, (t + 1) * BLOCK)
        if t == 0:
            kk = jnp.concatenate([kp_ref[:, ks], kc_ref[:BLOCK, ks]], axis=0)
            vv = jnp.concatenate([vp_ref[:, ks], vc_ref[:BLOCK, ks]], axis=0)
        else:
            kk = kc_ref[(t - 1) * BLOCK:(t + 1) * BLOCK, ks]
            vv = vc_ref[(t - 1) * BLOCK:(t + 1) * BLOCK, ks]
        heads = [h * SWA_GROUP + g for g in range(SWA_GROUP)]
        q4 = jnp.concatenate([q_ref[qr, hd * HEAD_DIM:(hd + 1) * HEAD_DIM] for hd in heads], axis=0)
        s = lax.dot_general(q4, kk, nt, preferred_element_type=F32)
        return jnp.where(valid[t], s, NEG_INF), vv, heads, qr

    @pl.when(bounded_ref[0] != 0)
    def _():
        for h in range(SWA_KV_HEADS):
            for t in range(SWA_STEP_BLOCKS):
                s, vv, heads, qr = masked_scores(h, t)
                vaug = jnp.concatenate([vv, ones], axis=1)
                o = jnp.dot(jnp.exp2(s).astype(BF16), vaug, preferred_element_type=F32)
                for g, hd in enumerate(heads):
                    rs = slice(g * BLOCK, (g + 1) * BLOCK)
                    sink = jnp.exp2(sinkrow_ref[hd:hd + 1, :] * LOG2E)
                    out = o[rs, :HEAD_DIM] / (o[rs, HEAD_DIM:] + sink)
                    o_ref[qr, hd * HEAD_DIM:(hd + 1) * HEAD_DIM] = out.astype(o_ref.dtype)

    @pl.when(bounded_ref[0] == 0)
    def _():
        for h in range(SWA_KV_HEADS):
            for t in range(SWA_STEP_BLOCKS):
                s, vv, heads, qr = masked_scores(h, t)
                for g, hd in enumerate(heads):
                    sg = s[g * BLOCK:(g + 1) * BLOCK]
                    sink = sink_ref[hd] * LOG2E
                    mx = jnp.maximum(jnp.max(sg, axis=-1, keepdims=True), sink)
                    p = jnp.exp2(sg - mx)
                    denom = jnp.sum(p, axis=-1, keepdims=True) + jnp.exp2(sink - mx)
                    o = jnp.dot(p.astype(BF16), vv, preferred_element_type=F32)
                    o_ref[qr, hd * HEAD_DIM:(hd + 1) * HEAD_DIM] = (o / denom).astype(o_ref.dtype)


def _swa_attention(proj, bounded, sinks, batch, seq):
    step_rows = SWA_STEP_BLOCKS * BLOCK
    ns = seq // step_rows
    qw = SWA_HEADS * HEAD_DIM
    kw = SWA_KV_HEADS * HEAD_DIM
    row = lambda b, j: b * ns + j
    prev = lambda b, j: (b * ns + j) * SWA_STEP_BLOCKS - jnp.minimum(j, 1)
    k_blk = SWA_Q_COLS // kw
    v_blk = k_blk + 1
    sink_rows = jnp.broadcast_to(sinks[:, None], (SWA_HEADS, HEAD_DIM))
    return pl.pallas_call(
        _swa_kernel,
        grid=(batch, ns),
        in_specs=[pl.BlockSpec(memory_space=pltpu.SMEM),
                  pl.BlockSpec(memory_space=pltpu.SMEM),
                  pl.BlockSpec((SWA_HEADS, HEAD_DIM), lambda b, j: (0, 0)),
                  pl.BlockSpec((step_rows, qw), lambda b, j: (row(b, j), 0)),
                  pl.BlockSpec((BLOCK, kw), lambda b, j: (prev(b, j), k_blk)),
                  pl.BlockSpec((step_rows, kw), lambda b, j: (row(b, j), k_blk)),
                  pl.BlockSpec((BLOCK, kw), lambda b, j: (prev(b, j), v_blk)),
                  pl.BlockSpec((step_rows, kw), lambda b, j: (row(b, j), v_blk))],
        out_specs=pl.BlockSpec((step_rows, qw), lambda b, j: (row(b, j), 0)),
        out_shape=jax.ShapeDtypeStruct((batch * seq, MIX_HALF), BF16),
        compiler_params=_params(("parallel", "parallel")),
        name="swa_attention",
    )(bounded, sinks, sink_rows, proj, proj, proj, proj, proj)


DIFF_TQ = 512
DIFF_TK = 256
DIFF_DIAG = DIFF_TQ // DIFF_TK
assert DIFF_DIAG * DIFF_TK == DIFF_TQ and DIFF_DIAG % 2 == 0
DIFF_RIDER_FROM = 2


def _diff_kernel(bounded_ref, lq1_ref, lk1_ref, lq2_ref, lk2_ref, subg_ref,
                 q1_ref, q2_ref, k1_ref, k2_ref, v_ref, wf_ref, o_ref, wbf_ref,
                 vaug_ref, acc_ref, m_ref, l_ref, accs_ref, *, lambda_init):
    qi = pl.program_id(2)
    pl.when(qi >= DIFF_RIDER_FROM)(lambda: _cast_chunk(wf_ref, wbf_ref))
    tq = DIFF_TQ
    nt = (((1,), (1,)), ((), ()))
    lane = lax.broadcasted_iota(jnp.int32, (tq, LANES), 1)
    qs = []
    for q_ref in (q1_ref, q2_ref):
        q = q_ref[...]
        for e in range(2):
            keep = (lane >= e * DIFF_QK_DIM) & (lane < (e + 1) * DIFF_QK_DIM)
            qs.append(jnp.where(keep, q, jnp.zeros_like(q)))

    tk = DIFF_TK
    k_refs = (k1_ref, k2_ref)
    row = lax.broadcasted_iota(jnp.int32, (tq, tk), 0)
    col = lax.broadcasted_iota(jnp.int32, (tq, tk), 1)
    causal = col <= row

    def scores(slot, j, r0, masked):
        off = pl.multiple_of(j * tk, tk)
        s = lax.dot_general(qs[slot][r0:], k_refs[slot // 2][pl.ds(off, tk), :], nt,
                            preferred_element_type=F32)
        if masked:
            s = jnp.where(causal[:tq - r0], s, NEG_INF)
        return s

    lam = (jnp.exp(jnp.sum(lq1_ref[...] * lk1_ref[...], axis=-1, keepdims=True))
           - jnp.exp(jnp.sum(lq2_ref[...] * lk2_ref[...], axis=-1, keepdims=True))
           + lambda_init)

    def finalize(e, o1, o2):
        o = o1 - lam * o2
        ms = jnp.mean(o * o, axis=-1, keepdims=True)
        on = o * lax.rsqrt(ms + NORM_EPS) * subg_ref[...]
        o_ref[:, e * HEAD_DIM:(e + 1) * HEAD_DIM] = (on * (1.0 - lambda_init)).astype(o_ref.dtype)

    @pl.when(qi == 0)
    def _():
        for e in range(2):
            vaug_ref[e, :, :HEAD_DIM] = v_ref[:, e * HEAD_DIM:(e + 1) * HEAD_DIM]
            vaug_ref[e, :, HEAD_DIM:] = jnp.ones((v_ref.shape[0], HEAD_DIM), BF16)

    @pl.when(bounded_ref[0] != 0)
    def _():
        def step(j, r0, masked, assign=False):
            off = pl.multiple_of(j * tk, tk)
            nr = tq - r0
            ps = []
            for mp in range(2):
                q2 = jnp.concatenate([qs[2 * mp][r0:], qs[2 * mp + 1][r0:]], axis=0)
                s = lax.dot_general(q2, k_refs[mp][pl.ds(off, tk), :], nt,
                                    preferred_element_type=F32)
                for e in range(2):
                    se = s[e * nr:(e + 1) * nr]
                    if masked:
                        se = jnp.where(causal[:nr], se, NEG_INF)
                    ps.append(jnp.exp2(se).astype(BF16))
            for e in range(2):
                pp = jnp.concatenate([ps[e], ps[2 + e]], axis=0)
                pv = jnp.dot(pp, vaug_ref[e, pl.ds(off, tk), :], preferred_element_type=F32)
                if assign:
                    acc_ref[e] = pv
                else:
                    acc_ref[e, r0:tq] += pv[:tq - r0]
                    acc_ref[e, tq + r0:] += pv[tq - r0:]

        step(DIFF_DIAG * qi, 0, True, assign=True)
        for d in range(1, DIFF_DIAG):
            step(DIFF_DIAG * qi + d, d * tk, True)

        def body(jj, carry):
            step(2 * jj, 0, False)
            step(2 * jj + 1, 0, False)
            return carry

        lax.fori_loop(0, (DIFF_DIAG * qi) // 2, body, 0)
        for e in range(2):
            a = acc_ref[e]
            finalize(e, a[:tq, :HEAD_DIM] / a[:tq, HEAD_DIM:],
                     a[tq:, :HEAD_DIM] / a[tq:, HEAD_DIM:])

    @pl.when(bounded_ref[0] == 0)
    def _():
        m_ref[...] = jnp.full(m_ref.shape, NEG_INF, F32)
        l_ref[...] = jnp.zeros(l_ref.shape, F32)
        accs_ref[...] = jnp.zeros(accs_ref.shape, F32)

        def step(j, r0, masked):
            off = pl.multiple_of(j * tk, tk)
            vb = v_ref[pl.ds(off, tk), :]
            for slot in range(4):
                e = slot % 2
                s = scores(slot, j, r0, masked)
                m_prev = m_ref[slot, r0:]
                m_new = jnp.maximum(m_prev, jnp.max(s, axis=-1, keepdims=True))
                alpha = jnp.exp2(m_prev - m_new)
                p = jnp.exp2(s - m_new)
                l_ref[slot, r0:] = alpha * l_ref[slot, r0:] + jnp.sum(p, axis=-1, keepdims=True)
                pv = jnp.dot(p.astype(BF16), vb[:, e * HEAD_DIM:(e + 1) * HEAD_DIM],
                             preferred_element_type=F32)
                accs_ref[slot, r0:] = alpha * accs_ref[slot, r0:] + pv
                m_ref[slot, r0:] = m_new

        def body(j, carry):
            step(j, 0, False)
            return carry

        lax.fori_loop(0, DIFF_DIAG * qi, body, 0)
        for d in range(DIFF_DIAG):
            step(DIFF_DIAG * qi + d, d * tk, True)
        for e in range(2):
            finalize(e, accs_ref[e] / l_ref[e], accs_ref[2 + e] / l_ref[2 + e])


def _diff_attention(proj, bounded, lq1, lk1, lq2, lk2, subg, w_next, batch, seq, lambda_init):
    tq = DIFF_TQ
    nq = seq // tq
    pairs = DIFF_HEADS // 2
    riders = nq - DIFF_RIDER_FROM
    chunks = batch * pairs * riders
    rows = w_next.shape[0] // chunks
    assert rows * chunks == w_next.shape[0] and rows % 16 == 0
    w_spec = pl.BlockSpec(
        (rows, w_next.shape[1]),
        lambda b, h, i: (jnp.maximum((b * pairs + h) * riders + jnp.maximum(i - DIFF_RIDER_FROM, -1), 0), 0))
    w_in_spec = w_out_spec = w_spec
    w_out_shape = jax.ShapeDtypeStruct(w_next.shape, BF16)
    map_blks = DIFF_HEADS * DIFF_QK_DIM // LANES
    vec = lambda d: pl.BlockSpec((1, d), lambda b, h, i: (0, 0))
    return pl.pallas_call(
        functools.partial(_diff_kernel, lambda_init=lambda_init),
        grid=(batch, pairs, nq),
        in_specs=[pl.BlockSpec(memory_space=pltpu.SMEM),
                  vec(DIFF_QK_DIM), vec(DIFF_QK_DIM), vec(DIFF_QK_DIM), vec(DIFF_QK_DIM),
                  vec(HEAD_DIM),
                  pl.BlockSpec((tq, LANES), lambda b, h, i: (b * nq + i, QB_BLK + h)),
                  pl.BlockSpec((tq, LANES), lambda b, h, i: (b * nq + i, QB_BLK + map_blks + h)),
                  pl.BlockSpec((seq, LANES), lambda b, h, i: (b, KB_BLK + h)),
                  pl.BlockSpec((seq, LANES), lambda b, h, i: (b, KB_BLK + map_blks + h)),
                  pl.BlockSpec((seq, 2 * HEAD_DIM), lambda b, h, i: (b, VB_BLK // 2 + h)),
                  w_in_spec],
        out_specs=[pl.BlockSpec((tq, 2 * HEAD_DIM), lambda b, h, i: (b * nq + i, h)), w_out_spec],
        out_shape=[jax.ShapeDtypeStruct((batch * seq, MIX_HALF), BF16), w_out_shape],
        scratch_shapes=[pltpu.VMEM((2, seq, 2 * HEAD_DIM), BF16),
                        pltpu.VMEM((2, 2 * tq, 2 * HEAD_DIM), F32),
                        pltpu.VMEM((4, tq, 1), F32),
                        pltpu.VMEM((4, tq, 1), F32),
                        pltpu.VMEM((4, tq, HEAD_DIM), F32)],
        compiler_params=_params(("arbitrary", "arbitrary", "arbitrary")),
        name="diff_attention",
    )(bounded, lq1.reshape(1, -1), lk1.reshape(1, -1), lq2.reshape(1, -1), lk2.reshape(1, -1),
      subg.reshape(1, -1), proj, proj, proj, proj, proj, w_next)


OUT_SUB_ROWS = 512


def _out_proj_kernel(oa_ref, ob_ref, w_ref, x_ref, g_ref, o_ref, hg_ref, ssq_ref):
    for s in range(o_ref.shape[0] // OUT_SUB_ROWS):
        rs = slice(s * OUT_SUB_ROWS, (s + 1) * OUT_SUB_ROWS)
        acc = jnp.dot(oa_ref[rs, :], w_ref[:MIX_HALF, :], preferred_element_type=F32)
        acc += jnp.dot(ob_ref[rs, :], w_ref[MIX_HALF:, :], preferred_element_type=F32)
        h = x_ref[rs, :] + acc
        o_ref[rs, :] = h
        hg_ref[rs, :] = (h * g_ref[...]).astype(hg_ref.dtype)
        ssq_ref[0, rs, :] = jnp.sum(h * h, axis=-1, keepdims=True)


def _out_proj(oa, ob, w_bf, x2d, g_next, tm=1024, tn=1024):
    m, d = x2d.shape
    tile = pl.BlockSpec((tm, tn), lambda n, i: (i, n))
    return pl.pallas_call(
        _out_proj_kernel,
        grid=(d // tn, m // tm),
        in_specs=[pl.BlockSpec((tm, MIX_HALF), lambda n, i: (i, 0)),
                  pl.BlockSpec((tm, MIX_HALF), lambda n, i: (i, 0)),
                  pl.BlockSpec((2 * MIX_HALF, tn), lambda n, i: (0, n)),
                  tile,
                  pl.BlockSpec((1, tn), lambda n, i: (0, n))],
        out_specs=[tile, tile, pl.BlockSpec((1, tm, 1), lambda n, i: (n, i, 0))],
        out_shape=[jax.ShapeDtypeStruct((m, d), F32),
                   jax.ShapeDtypeStruct((m, d), BF16),
                   jax.ShapeDtypeStruct((d // tn, m, 1), F32)],
        compiler_params=_params(("parallel", "parallel")),
        name="out_proj",
    )(oa, ob, w_bf, x2d, g_next.reshape(1, d))


def _up_kernel(a_ref, ssq_ref, w_ref, wf1_ref, wf2_ref, o_ref, wbf1_ref, wbf2_ref):
    _cast_chunk(wf1_ref, wbf1_ref)
    _cast_chunk(wf2_ref, wbf2_ref)
    k = a_ref.shape[1]
    for s in range(o_ref.shape[0] // OUT_SUB_ROWS):
        rs = slice(s * OUT_SUB_ROWS, (s + 1) * OUT_SUB_ROWS)
        ms = jnp.sum(ssq_ref[:, rs, :], axis=0) * (1.0 / k)
        inv_rms = lax.rsqrt(ms + NORM_EPS)
        acc = jnp.dot(a_ref[rs, :], w_ref[...], preferred_element_type=F32) * inv_rms
        o_ref[rs, :] = jnp.square(jnp.maximum(acc, 0.0)).astype(o_ref.dtype)


def _mlp_up(a, ssq, w_bf, w_next1, w_next2, tm=1024, tn=1024):
    m, k = a.shape
    n = w_bf.shape[1]
    grid = (n // tn, m // tm)
    in1, out1, shape1 = _cast_rider(w_next1, grid)
    in2, out2, shape2 = _cast_rider(w_next2, grid)
    return pl.pallas_call(
        _up_kernel,
        grid=grid,
        in_specs=[pl.BlockSpec((tm, k), lambda j, i: (i, 0)),
                  pl.BlockSpec((ssq.shape[0], tm, 1), lambda j, i: (0, i, 0)),
                  pl.BlockSpec((k, tn), lambda j, i: (0, j)),
                  in1, in2],
        out_specs=[pl.BlockSpec((tm, tn), lambda j, i: (i, j)), out1, out2],
        out_shape=[jax.ShapeDtypeStruct((m, n), BF16), shape1, shape2],
        compiler_params=_params(("parallel", "parallel")),
        name="mlp_up",
    )(a, ssq, w_bf, w_next1, w_next2)


DOWN_SUB_ROWS = 512


def _down_kernel(u_ref, w_ref, h_ref, o_ref, ob_ref):
    kk = pl.program_id(2)

    def accumulate(first):
        for s in range(o_ref.shape[0] // DOWN_SUB_ROWS):
            rs = slice(s * DOWN_SUB_ROWS, (s + 1) * DOWN_SUB_ROWS)
            part = jnp.dot(u_ref[rs, :], w_ref[...], preferred_element_type=F32)
            o_ref[rs, :] = (h_ref[rs, :] if first else o_ref[rs, :]) + part

    pl.when(kk == 0)(lambda: accumulate(True))
    pl.when(kk > 0)(lambda: accumulate(False))

    @pl.when(kk == pl.num_programs(2) - 1)
    def _():
        ob_ref[...] = o_ref[...].astype(ob_ref.dtype)


def _mlp_down(u, w_bf, h2d, tm=1024, tn=1024, tk=4096):
    m, k = u.shape
    d = w_bf.shape[1]
    return pl.pallas_call(
        _down_kernel,
        grid=(d // tn, m // tm, k // tk),
        in_specs=[pl.BlockSpec((tm, tk), lambda j, i, kk: (i, kk)),
                  pl.BlockSpec((tk, tn), lambda j, i, kk: (kk, j)),
                  pl.BlockSpec((tm, tn), lambda j, i, kk: (i, j))],
        out_specs=[pl.BlockSpec((tm, tn), lambda j, i, kk: (i, j)),
                   pl.BlockSpec((tm, tn), lambda j, i, kk: (i, j))],
        out_shape=[jax.ShapeDtypeStruct((m, d), F32),
                   jax.ShapeDtypeStruct((m, d), BF16)],
        compiler_params=_params(("parallel", "parallel", "arbitrary")),
        name="mlp_down",
    )(u, w_bf, h2d)


def _ple_kernel(p_ref, w_ref, g_ref, o_ref):
    y = jnp.dot(p_ref[...].astype(BF16), w_ref[...], preferred_element_type=F32)
    ms = jnp.mean(y * y, axis=-1, keepdims=True)
    o_ref[...] = y * lax.rsqrt(ms + NORM_EPS) * g_ref[...]


def _ple_embed(p2d, w_bf, g, tm=512):
    m, k = p2d.shape
    d = w_bf.shape[1]
    return pl.pallas_call(
        _ple_kernel,
        grid=(m // tm,),
        in_specs=[pl.BlockSpec((tm, k), lambda i: (i, 0)),
                  pl.BlockSpec((k, d), lambda i: (0, 0)),
                  pl.BlockSpec((1, d), lambda i: (0, 0))],
        out_specs=pl.BlockSpec((tm, d), lambda i: (i, 0)),
        out_shape=jax.ShapeDtypeStruct((m, d), F32),
        compiler_params=_params(("parallel",)),
        name="ple_embed",
    )(p2d, w_bf, g.reshape(1, d))


def _gate_kernel(hb_ref, w_ref, h_ref, pe_ref, o_ref):
    for s in range(o_ref.shape[0] // OUT_SUB_ROWS):
        rs = slice(s * OUT_SUB_ROWS, (s + 1) * OUT_SUB_ROWS)
        z = jnp.dot(hb_ref[rs, :], w_ref[...], preferred_element_type=F32)
        o_ref[rs, :] = h_ref[rs, :] + jax.nn.sigmoid(z) * pe_ref[rs, :]


def _ple_gate(hb, w_bf, h2d, pe, tm=512, tn=1024):
    m, d = h2d.shape
    tile = pl.BlockSpec((tm, tn), lambda j, i: (i, j))
    return pl.pallas_call(
        _gate_kernel,
        grid=(d // tn, m // tm),
        in_specs=[pl.BlockSpec((tm, d), lambda j, i: (i, 0)),
                  pl.BlockSpec((d, tn), lambda j, i: (0, j)),
                  tile, tile],
        out_specs=tile,
        out_shape=jax.ShapeDtypeStruct((m, d), F32),
        compiler_params=_params(("parallel", "parallel")),
        name="ple_gate",
    )(hb, w_bf, h2d, pe)


def _rope_tables(seq):
    pos = jnp.arange(seq, dtype=F32)

    def tab(dim):
        inv = 1.0 / (ROPE_THETA ** (jnp.arange(0, dim, 2, dtype=F32) / dim))
        ang = pos[:, None] * inv[None, :]
        ang = jnp.concatenate([ang, ang], axis=-1)
        return jnp.cos(ang), jnp.sin(ang)

    lane = jnp.arange(LANES)
    cos_a, sin_a = tab(HEAD_DIM)
    sin_a = jnp.where(lane < HEAD_DIM // 2, -sin_a, sin_a)
    cos_b, sin_b = tab(DIFF_QK_DIM)
    cos_b = jnp.concatenate([cos_b, cos_b], axis=-1)
    sin_b = jnp.concatenate([sin_b, sin_b], axis=-1)
    first_half = (lane % DIFF_QK_DIM) < DIFF_QK_DIM // 2
    sin_b_lo = jnp.where(first_half, -sin_b, 0.0)
    sin_b_hi = jnp.where(first_half, 0.0, sin_b)
    return cos_a, sin_a, cos_b, sin_b_lo, sin_b_hi


def _gain_rows(swa_q_g, swa_k_g, diff_q_g, diff_k_g):
    qa = jnp.tile(swa_q_g, SWA_Q_COLS // HEAD_DIM) * (LOG2E / math.sqrt(HEAD_DIM))
    ka = jnp.tile(swa_k_g, SWA_KV_COLS // HEAD_DIM)
    qb = jnp.tile(diff_q_g, DIFF_QK_COLS // DIFF_QK_DIM) * (LOG2E / math.sqrt(DIFF_QK_DIM))
    kb = jnp.tile(diff_k_g, DIFF_QK_COLS // DIFF_QK_DIM)
    cols = jnp.concatenate([qa, ka, jnp.ones((SWA_KV_COLS,), F32), qb, kb,
                            jnp.ones((DIFF_V_COLS,), F32)])
    return cols.reshape(IN_TILES, 1, IN_TN)


def _scores_bounded(q_g, k_g, dim, sinks=None):
    rounding_slack = 1.02
    bound = (math.sqrt(dim) * LOG2E * rounding_slack
             * jnp.max(jnp.abs(q_g)) * jnp.max(jnp.abs(k_g)))
    if sinks is not None:
        bound = jnp.maximum(bound, LOG2E * jnp.max(jnp.abs(sinks)))
    return (bound <= SCORE_BOUND).astype(jnp.int32).reshape(1)


def kernel(x, p, attn_norm_g, w_in, swa_q_norm_g, swa_k_norm_g, swa_sinks, diff_q_norm_g, diff_k_norm_g, diff_lambda_q1, diff_lambda_k1, diff_lambda_q2, diff_lambda_k2, diff_subln_g, w_o, mlp_norm_g, w_up, w_down, w_ple_proj, ple_norm_g, w_ple_gate):
    batch, seq, d = x.shape
    depth = w_in.shape[0]
    tables = _rope_tables(seq)
    h = x.reshape(batch * seq, d)
    for i in range(depth):
        lambda_init = 0.8 - 0.6 * math.exp(-0.3 * i)
        a = _rmsnorm_rows(h, attn_norm_g[i])
        gains = _gain_rows(swa_q_norm_g[i], swa_k_norm_g[i], diff_q_norm_g[i], diff_k_norm_g[i])
        proj, w_o_bf = _in_proj(a, w_in[i], gains, tables, w_o[i], seq)
        bounded_a = _scores_bounded(swa_q_norm_g[i], swa_k_norm_g[i], HEAD_DIM, swa_sinks[i])
        out_a = _swa_attention(proj, bounded_a, swa_sinks[i], batch, seq)
        bounded_b = _scores_bounded(diff_q_norm_g[i], diff_k_norm_g[i], DIFF_QK_DIM)
        out_b, w_up_bf = _diff_attention(proj, bounded_b, diff_lambda_q1[i], diff_lambda_k1[i],
                                         diff_lambda_q2[i], diff_lambda_k2[i], diff_subln_g[i],
                                         w_up[i], batch, seq, lambda_init)
        h, hg, ssq = _out_proj(out_a, out_b, w_o_bf, h, mlp_norm_g[i])
        u, w_down_bf, w_gate_bf = _mlp_up(hg, ssq, w_up_bf, w_down[i], w_ple_gate[i])
        h, hb = _mlp_down(u, w_down_bf, h)
        pe = _ple_embed(p[i].reshape(batch * seq, PLE_DIM), w_ple_proj[i].astype(BF16), ple_norm_g[i])
        h = _ple_gate(hb, w_gate_bf, h, pe)
    return h.reshape(batch, seq, d)
```

```python
import functools
import math

import jax
import jax.numpy as jnp
from jax import lax
from jax.experimental import pallas as pl
from jax.experimental.pallas import tpu as pltpu

F32 = jnp.float32
BF16 = jnp.bfloat16

D_MODEL = 4096
HEAD_DIM = 128
SWA_HEADS = 16
SWA_KV_HEADS = 4
SWA_GROUP = SWA_HEADS // SWA_KV_HEADS
BLOCK = 128
DIFF_HEADS = 16
DIFF_QK_DIM = 64
D_FF = 4 * D_MODEL
PLE_DIM = 256
ROPE_THETA = 10000.0
NORM_EPS = 1e-6
NEG_INF = -1e30
LOG2E = math.log2(math.e)
SCORE_BOUND = 40.0

SWA_Q_COLS = SWA_HEADS * HEAD_DIM
SWA_KV_COLS = SWA_KV_HEADS * HEAD_DIM
DIFF_QK_COLS = 2 * DIFF_HEADS * DIFF_QK_DIM
DIFF_V_COLS = DIFF_HEADS * HEAD_DIM
IN_COLS = SWA_Q_COLS + 2 * SWA_KV_COLS + 2 * DIFF_QK_COLS + DIFF_V_COLS
MIX_HALF = SWA_Q_COLS

LANES = 128
VMEM_LIMIT = 56 * 1024 * 1024

KA_BLK = SWA_Q_COLS // LANES
VA_BLK = KA_BLK + SWA_KV_COLS // LANES
QB_BLK = VA_BLK + SWA_KV_COLS // LANES
KB_BLK = QB_BLK + DIFF_QK_COLS // LANES
VB_BLK = KB_BLK + DIFF_QK_COLS // LANES

IN_TN = 1024
IN_TILES = IN_COLS // IN_TN
IN_HALF = IN_TN // 2
IN_SUB_ROWS = 256
IN_RIDER_ROUNDS = IN_TILES - 1
_HALF_KINDS = (['a'] * ((SWA_Q_COLS + SWA_KV_COLS) // IN_HALF) + ['p'] * (SWA_KV_COLS // IN_HALF)
               + ['b'] * (2 * DIFF_QK_COLS // IN_HALF) + ['p'] * (DIFF_V_COLS // IN_HALF))
_TILE_KINDS = [(_HALF_KINDS[2 * t], _HALF_KINDS[2 * t + 1]) for t in range(IN_TILES)]


def _params(sem):
    return pltpu.CompilerParams(dimension_semantics=sem, vmem_limit_bytes=VMEM_LIMIT)


def _cast_rider(w, grid):
    steps = math.prod(grid)
    rows = w.shape[0] // steps
    assert rows * steps == w.shape[0] and rows % 16 == 0, (w.shape, grid)

    def index_map(*idx):
        step = idx[0]
        for size, i in zip(grid[1:], idx[1:]):
            step = step * size + i
        return (step, 0)

    spec = pl.BlockSpec((rows, w.shape[1]), index_map)
    return spec, spec, jax.ShapeDtypeStruct(w.shape, BF16)


def _cast_chunk(wf_ref, wbf_ref):
    wbf_ref[...] = wf_ref[...].astype(BF16)


def _rmsnorm_kernel(x_ref, g_ref, o_ref):
    x = x_ref[...]
    ms = jnp.mean(x * x, axis=-1, keepdims=True)
    o_ref[...] = (x * lax.rsqrt(ms + NORM_EPS) * g_ref[...]).astype(o_ref.dtype)


def _rmsnorm_rows(x2d, g, tm=512):
    m, d = x2d.shape
    return pl.pallas_call(
        _rmsnorm_kernel,
        grid=(m // tm,),
        in_specs=[pl.BlockSpec((tm, d), lambda i: (i, 0)),
                  pl.BlockSpec((1, d), lambda i: (0, 0))],
        out_specs=pl.BlockSpec((tm, d), lambda i: (i, 0)),
        out_shape=jax.ShapeDtypeStruct((m, d), BF16),
        compiler_params=_params(("parallel",)),
        name="rmsnorm_rows",
    )(x2d, g.reshape(1, d))


def _in_proj_kernel(a_ref, wf_ref, g_ref, ca_ref, sa_ref, cb_ref, sbl_ref, sbh_ref, nextf_ref,
                    o_ref, nextbf_ref, w_ref):
    r = pl.program_id(0)
    i = pl.program_id(1)
    tm = a_ref.shape[0]
    chunk = wf_ref.shape[0]

    @pl.when((r >= 1) & (r <= IN_RIDER_ROUNDS))
    def _():
        _cast_chunk(nextf_ref, nextbf_ref)

    @pl.when(r < IN_TILES)
    def _():
        w_ref[r % 2, pl.ds(pl.multiple_of(i * chunk, chunk), chunk), :] = wf_ref[...].astype(BF16)

    def plain(acc, rs, c0):
        o_ref[rs, c0:c0 + IN_HALF] = acc.astype(o_ref.dtype)

    def heads128(acc, rs, c0):
        for c in range(IN_HALF // LANES):
            sl = slice(c * LANES, (c + 1) * LANES)
            osl = slice(c0 + c * LANES, c0 + (c + 1) * LANES)
            y = acc[:, sl]
            ms = jnp.mean(y * y, axis=-1, keepdims=True)
            yn = y * lax.rsqrt(ms + NORM_EPS) * g_ref[0][:, osl]
            out = yn * ca_ref[rs, :] + pltpu.roll(yn, HEAD_DIM // 2, 1) * sa_ref[rs, :]
            o_ref[rs, osl] = out.astype(o_ref.dtype)

    def heads64(acc, rs, c0):
        lane = lax.broadcasted_iota(jnp.int32, (1, LANES), 1)
        lo = lane < DIFF_QK_DIM
        for c in range(IN_HALF // LANES):
            sl = slice(c * LANES, (c + 1) * LANES)
            osl = slice(c0 + c * LANES, c0 + (c + 1) * LANES)
            y = acc[:, sl]
            sq = y * y
            s_lo = jnp.sum(jnp.where(lo, sq, 0.0), axis=-1, keepdims=True)
            s_hi = jnp.sum(jnp.where(lo, 0.0, sq), axis=-1, keepdims=True)
            ms = jnp.where(lo, s_lo, s_hi) * (1.0 / DIFF_QK_DIM)
            yn = y * lax.rsqrt(ms + NORM_EPS) * g_ref[0][:, osl]
            out = (yn * cb_ref[rs, :]
                   + pltpu.roll(yn, LANES - DIFF_QK_DIM // 2, 1) * sbl_ref[rs, :]
                   + pltpu.roll(yn, DIFF_QK_DIM // 2, 1) * sbh_ref[rs, :])
            o_ref[rs, osl] = out.astype(o_ref.dtype)

    epilogues = {'a': heads128, 'b': heads64, 'p': plain}

    def run(kinds):
        slot = (r - 1) % 2
        for s in range(tm // IN_SUB_ROWS):
            rs = slice(s * IN_SUB_ROWS, (s + 1) * IN_SUB_ROWS)
            acc = jnp.dot(a_ref[rs, :], w_ref[slot], preferred_element_type=F32)
            for half, kind in enumerate(kinds):
                c0 = half * IN_HALF
                epilogues[kind](acc[:, c0:c0 + IN_HALF], rs, c0)

    for kinds in sorted(set(_TILE_KINDS)):
        tiles = [t for t in range(IN_TILES) if _TILE_KINDS[t] == kinds]
        lo_t, hi_t = tiles[0], tiles[-1]
        assert tiles == list(range(lo_t, hi_t + 1))
        pl.when((r - 1 >= lo_t) & (r - 1 <= hi_t))(functools.partial(run, kinds))


def _in_proj(a, w, gains, tables, w_next, seq, tm=1024):
    m, k = a.shape
    steps = m // tm
    pos_blocks = seq // tm
    last = IN_TILES - 1
    rider_chunks = IN_RIDER_ROUNDS * steps
    rider_rows = w_next.shape[0] // rider_chunks
    assert rider_rows * rider_chunks == w_next.shape[0] and rider_rows % 16 == 0
    row = lambda r, i: jnp.where(r == 0, 0, i)
    tile = lambda r, i: jnp.maximum(r - 1, 0)
    tab_spec = pl.BlockSpec((tm, LANES), lambda r, i: (i % pos_blocks, 0))
    rider_spec = pl.BlockSpec((rider_rows, w_next.shape[1]),
                              lambda r, i: (jnp.clip((r - 1) * steps + i, 0, rider_chunks - 1), 0))
    return pl.pallas_call(
        _in_proj_kernel,
        grid=(IN_TILES + 1, steps),
        in_specs=[pl.BlockSpec((tm, k), lambda r, i: (row(r, i), 0)),
                  pl.BlockSpec((k // steps, IN_TN), lambda r, i: (i, jnp.minimum(r, last))),
                  pl.BlockSpec((1, 1, IN_TN), lambda r, i: (tile(r, i), 0, 0)),
                  tab_spec, tab_spec, tab_spec, tab_spec, tab_spec, rider_spec],
        out_specs=[pl.BlockSpec((tm, IN_TN), lambda r, i: (row(r, i), tile(r, i))), rider_spec],
        out_shape=[jax.ShapeDtypeStruct((m, IN_COLS), BF16),
                   jax.ShapeDtypeStruct(w_next.shape, BF16)],
        scratch_shapes=[pltpu.VMEM((2, k, IN_TN), BF16)],
        compiler_params=_params(("arbitrary", "arbitrary")),
        name="in_proj",
    )(a, w, gains, *tables, w_next)


SWA_STEP_BLOCKS = 2


def _swa_kernel(bounded_ref, sink_ref, sinkrow_ref, q_ref, kp_ref, kc_ref, vp_ref, vc_ref, o_ref):
    j = pl.program_id(1)
    nt = (((1,), (1,)), ((), ()))
    rows = SWA_GROUP * BLOCK
    r = lax.broadcasted_iota(jnp.int32, (rows, 2 * BLOCK), 0) & (BLOCK - 1)
    c = lax.broadcasted_iota(jnp.int32, (rows, 2 * BLOCK), 1)
    window = (c > r) & (c <= r + BLOCK)
    valid = [window & ((j > 0) | (c >= BLOCK))] + [window] * (SWA_STEP_BLOCKS - 1)
    ones = jnp.ones((2 * BLOCK, HEAD_DIM), BF16)

    def masked_scores(h, t):
        ks = slice(h * HEAD_DIM, (h + 1) * HEAD_DIM)
        qr = slice(t * BLOCK, (t + 1) * BLOCK)
        if t == 0:
            kk = jnp.concatenate([kp_ref[:, ks], kc_ref[:BLOCK, ks]], axis=0)
            vv = jnp.concatenate([vp_ref[:, ks], vc_ref[:BLOCK, ks]], axis=0)
        else:
            kk = kc_ref[(t - 1) * BLOCK:(t + 1) * BLOCK, ks]
            vv = vc_ref[(t - 1) * BLOCK:(t + 1) * BLOCK, ks]
        heads = [h * SWA_GROUP + g for g in range(SWA_GROUP)]
        q4 = jnp.concatenate([q_ref[qr, hd * HEAD_DIM:(hd + 1) * HEAD_DIM] for hd in heads], axis=0)
        s = lax.dot_general(q4, kk, nt, preferred_element_type=F32)
        return jnp.where(valid[t], s, NEG_INF), vv, heads, qr

    @pl.when(bounded_ref[0] != 0)
    def _():
        for h in range(SWA_KV_HEADS):
            for t in range(SWA_STEP_BLOCKS):
                s, vv, heads, qr = masked_scores(h, t)
                vaug = jnp.concatenate([vv, ones], axis=1)
                o = jnp.dot(jnp.exp2(s).astype(BF16), vaug, preferred_element_type=F32)
                for g, hd in enumerate(heads):
                    rs = slice(g * BLOCK, (g + 1) * BLOCK)
                    sink = jnp.exp2(sinkrow_ref[hd:hd + 1, :] * LOG2E)
                    out = o[rs, :HEAD_DIM] / (o[rs, HEAD_DIM:] + sink)
                    o_ref[qr, hd * HEAD_DIM:(hd + 1) * HEAD_DIM] = out.astype(o_ref.dtype)

    @pl.when(bounded_ref[0] == 0)
    def _():
        for h in range(SWA_KV_HEADS):
            for t in range(SWA_STEP_BLOCKS):
                s, vv, heads, qr = masked_scores(h, t)
                for g, hd in enumerate(heads):
                    sg = s[g * BLOCK:(g + 1) * BLOCK]
                    sink = sink_ref[hd] * LOG2E
                    mx = jnp.maximum(jnp.max(sg, axis=-1, keepdims=True), sink)
                    p = jnp.exp2(sg - mx)
                    denom = jnp.sum(p, axis=-1, keepdims=True) + jnp.exp2(sink - mx)
                    o = jnp.dot(p.astype(BF16), vv, preferred_element_type=F32)
                    o_ref[qr, hd * HEAD_DIM:(hd + 1) * HEAD_DIM] = (o / denom).astype(o_ref.dtype)


def _swa_attention(proj, bounded, sinks, batch, seq):
    step_rows = SWA_STEP_BLOCKS * BLOCK
    ns = seq // step_rows
    qw = SWA_HEADS * HEAD_DIM
    kw = SWA_KV_HEADS * HEAD_DIM
    row = lambda b, j: b * ns + j
    prev = lambda b, j: (b * ns + j) * SWA_STEP_BLOCKS - jnp.minimum(j, 1)
    k_blk = SWA_Q_COLS // kw
    v_blk = k_blk + 1
    sink_rows = jnp.broadcast_to(sinks[:, None], (SWA_HEADS, HEAD_DIM))
    return pl.pallas_call(
        _swa_kernel,
        grid=(batch, ns),
        in_specs=[pl.BlockSpec(memory_space=pltpu.SMEM),
                  pl.BlockSpec(memory_space=pltpu.SMEM),
                  pl.BlockSpec((SWA_HEADS, HEAD_DIM), lambda b, j: (0, 0)),
                  pl.BlockSpec((step_rows, qw), lambda b, j: (row(b, j), 0)),
                  pl.BlockSpec((BLOCK, kw), lambda b, j: (prev(b, j), k_blk)),
                  pl.BlockSpec((step_rows, kw), lambda b, j: (row(b, j), k_blk)),
                  pl.BlockSpec((BLOCK, kw), lambda b, j: (prev(b, j), v_blk)),
                  pl.BlockSpec((step_rows, kw), lambda b, j: (row(b, j), v_blk))],
        out_specs=pl.BlockSpec((step_rows, qw), lambda b, j: (row(b, j), 0)),
        out_shape=jax.ShapeDtypeStruct((batch * seq, MIX_HALF), BF16),
        compiler_params=_params(("parallel", "parallel")),
        name="swa_attention",
    )(bounded, sinks, sink_rows, proj, proj, proj, proj, proj)


DIFF_TQ = 1024
DIFF_TK = 256
DIFF_DIAG = DIFF_TQ // DIFF_TK
assert DIFF_DIAG * DIFF_TK == DIFF_TQ and DIFF_DIAG % 2 == 0

def _diff_kernel(bounded_ref, lq1_ref, lk1_ref, lq2_ref, lk2_ref, subg_ref,
                 q1_ref, q2_ref, k1_ref, k2_ref, v_ref, wf_ref, o_ref, wbf_ref,
                 vaug_ref, acc_ref, m_ref, l_ref, accs_ref, *, lambda_init):
    _cast_chunk(wf_ref, wbf_ref)
    qi = pl.program_id(2)
    tq = DIFF_TQ
    nt = (((1,), (1,)), ((), ()))
    lane = lax.broadcasted_iota(jnp.int32, (tq, LANES), 1)
    qs = []
    for q_ref in (q1_ref, q2_ref):
        q = q_ref[...]
        for e in range(2):
            keep = (lane >= e * DIFF_QK_DIM) & (lane < (e + 1) * DIFF_QK_DIM)
            qs.append(jnp.where(keep, q, jnp.zeros_like(q)))

    tk = DIFF_TK
    k_refs = (k1_ref, k2_ref)
    row = lax.broadcasted_iota(jnp.int32, (tq, tk), 0)
    col = lax.broadcasted_iota(jnp.int32, (tq, tk), 1)
    causal = col <= row

    def scores(slot, j, r0, masked):
        off = pl.multiple_of(j * tk, tk)
        s = lax.dot_general(qs[slot][r0:], k_refs[slot // 2][pl.ds(off, tk), :], nt,
                            preferred_element_type=F32)
        if masked:
            s = jnp.where(causal[:tq - r0], s, NEG_INF)
        return s

    lam = (jnp.exp(jnp.sum(lq1_ref[...] * lk1_ref[...], axis=-1, keepdims=True))
           - jnp.exp(jnp.sum(lq2_ref[...] * lk2_ref[...], axis=-1, keepdims=True))
           + lambda_init)

    def finalize(e, o1, o2):
        o = o1 - lam * o2
        ms = jnp.mean(o * o, axis=-1, keepdims=True)
        on = o * lax.rsqrt(ms + NORM_EPS) * subg_ref[...]
        o_ref[:, e * HEAD_DIM:(e + 1) * HEAD_DIM] = (on * (1.0 - lambda_init)).astype(o_ref.dtype)

    @pl.when(qi == 0)
    def _():
        for e in range(2):
            vaug_ref[e, :, :HEAD_DIM] = v_ref[:, e * HEAD_DIM:(e + 1) * HEAD_DIM]
            vaug_ref[e, :, HEAD_DIM:] = jnp.ones((v_ref.shape[0], HEAD_DIM), BF16)

    @pl.when(bounded_ref[0] != 0)
    def _():
        def step(j, r0, masked, assign=False):
            off = pl.multiple_of(j * tk, tk)
            ps = [jnp.exp2(scores(slot, j, r0, masked)).astype(BF16) for slot in range(4)]
            for e in range(2):
                pp = jnp.concatenate([ps[e], ps[2 + e]], axis=0)
                pv = jnp.dot(pp, vaug_ref[e, pl.ds(off, tk), :], preferred_element_type=F32)
                if assign:
                    acc_ref[e] = pv
                else:
                    acc_ref[e, r0:tq] += pv[:tq - r0]
                    acc_ref[e, tq + r0:] += pv[tq - r0:]

        step(DIFF_DIAG * qi, 0, True, assign=True)
        for d in range(1, DIFF_DIAG):
            step(DIFF_DIAG * qi + d, d * tk, True)

        def body(jj, carry):
            step(2 * jj, 0, False)
            step(2 * jj + 1, 0, False)
            return carry

        lax.fori_loop(0, (DIFF_DIAG * qi) // 2, body, 0)
        for e in range(2):
            a = acc_ref[e]
            finalize(e, a[:tq, :HEAD_DIM] / a[:tq, HEAD_DIM:],
                     a[tq:, :HEAD_DIM] / a[tq:, HEAD_DIM:])

    @pl.when(bounded_ref[0] == 0)
    def _():
        m_ref[...] = jnp.full(m_ref.shape, NEG_INF, F32)
        l_ref[...] = jnp.zeros(l_ref.shape, F32)
        accs_ref[...] = jnp.zeros(accs_ref.shape, F32)

        def step(j, r0, masked):
            off = pl.multiple_of(j * tk, tk)
            vb = v_ref[pl.ds(off, tk), :]
            for slot in range(4):
                e = slot % 2
                s = scores(slot, j, r0, masked)
                m_prev = m_ref[slot, r0:]
                m_new = jnp.maximum(m_prev, jnp.max(s, axis=-1, keepdims=True))
                alpha = jnp.exp2(m_prev - m_new)
                p = jnp.exp2(s - m_new)
                l_ref[slot, r0:] = alpha * l_ref[slot, r0:] + jnp.sum(p, axis=-1, keepdims=True)
                pv = jnp.dot(p.astype(BF16), vb[:, e * HEAD_DIM:(e + 1) * HEAD_DIM],
                             preferred_element_type=F32)
                accs_ref[slot, r0:] = alpha * accs_ref[slot, r0:] + pv
                m_ref[slot, r0:] = m_new

        def body(j, carry):
            step(j, 0, False)
            return carry

        lax.fori_loop(0, DIFF_DIAG * qi, body, 0)
        for d in range(DIFF_DIAG):
            step(DIFF_DIAG * qi + d, d * tk, True)
        for e in range(2):
            finalize(e, accs_ref[e] / l_ref[e], accs_ref[2 + e] / l_ref[2 + e])


def _diff_attention(proj, bounded, lq1, lk1, lq2, lk2, subg, w_next, batch, seq, lambda_init):
    tq = DIFF_TQ
    nq = seq // tq
    pairs = DIFF_HEADS // 2
    w_in_spec, w_out_spec, w_out_shape = _cast_rider(w_next, (batch, pairs, nq))
    map_blks = DIFF_HEADS * DIFF_QK_DIM // LANES
    vec = lambda d: pl.BlockSpec((1, d), lambda b, h, i: (0, 0))
    return pl.pallas_call(
        functools.partial(_diff_kernel, lambda_init=lambda_init),
        grid=(batch, pairs, nq),
        in_specs=[pl.BlockSpec(memory_space=pltpu.SMEM),
                  vec(DIFF_QK_DIM), vec(DIFF_QK_DIM), vec(DIFF_QK_DIM), vec(DIFF_QK_DIM),
                  vec(HEAD_DIM),
                  pl.BlockSpec((tq, LANES), lambda b, h, i: (b * nq + i, QB_BLK + h)),
                  pl.BlockSpec((tq, LANES), lambda b, h, i: (b * nq + i, QB_BLK + map_blks + h)),
                  pl.BlockSpec((seq, LANES), lambda b, h, i: (b, KB_BLK + h)),
                  pl.BlockSpec((seq, LANES), lambda b, h, i: (b, KB_BLK + map_blks + h)),
                  pl.BlockSpec((seq, 2 * HEAD_DIM), lambda b, h, i: (b, VB_BLK // 2 + h)),
                  w_in_spec],
        out_specs=[pl.BlockSpec((tq, 2 * HEAD_DIM), lambda b, h, i: (b * nq + i, h)), w_out_spec],
        out_shape=[jax.ShapeDtypeStruct((batch * seq, MIX_HALF), BF16), w_out_shape],
        scratch_shapes=[pltpu.VMEM((2, seq, 2 * HEAD_DIM), BF16),
                        pltpu.VMEM((2, 2 * tq, 2 * HEAD_DIM), F32),
                        pltpu.VMEM((4, tq, 1), F32),
                        pltpu.VMEM((4, tq, 1), F32),
                        pltpu.VMEM((4, tq, HEAD_DIM), F32)],
        compiler_params=_params(("parallel", "parallel", "arbitrary")),
        name="diff_attention",
    )(bounded, lq1.reshape(1, -1), lk1.reshape(1, -1), lq2.reshape(1, -1), lk2.reshape(1, -1),
      subg.reshape(1, -1), proj, proj, proj, proj, proj, w_next)


OUT_SUB_ROWS = 512


def _out_proj_kernel(oa_ref, ob_ref, w_ref, x_ref, g_ref, o_ref, hg_ref, ssq_ref):
    for s in range(o_ref.shape[0] // OUT_SUB_ROWS):
        rs = slice(s * OUT_SUB_ROWS, (s + 1) * OUT_SUB_ROWS)
        acc = jnp.dot(oa_ref[rs, :], w_ref[:MIX_HALF, :], preferred_element_type=F32)
        acc += jnp.dot(ob_ref[rs, :], w_ref[MIX_HALF:, :], preferred_element_type=F32)
        h = x_ref[rs, :] + acc
        o_ref[rs, :] = h
        hg_ref[rs, :] = (h * g_ref[...]).astype(hg_ref.dtype)
        ssq_ref[0, rs, :] = jnp.sum(h * h, axis=-1, keepdims=True)


def _out_proj(oa, ob, w_bf, x2d, g_next, tm=1024, tn=1024):
    m, d = x2d.shape
    tile = pl.BlockSpec((tm, tn), lambda n, i: (i, n))
    return pl.pallas_call(
        _out_proj_kernel,
        grid=(d // tn, m // tm),
        in_specs=[pl.BlockSpec((tm, MIX_HALF), lambda n, i: (i, 0)),
                  pl.BlockSpec((tm, MIX_HALF), lambda n, i: (i, 0)),
                  pl.BlockSpec((2 * MIX_HALF, tn), lambda n, i: (0, n)),
                  tile,
                  pl.BlockSpec((1, tn), lambda n, i: (0, n))],
        out_specs=[tile, tile, pl.BlockSpec((1, tm, 1), lambda n, i: (n, i, 0))],
        out_shape=[jax.ShapeDtypeStruct((m, d), F32),
                   jax.ShapeDtypeStruct((m, d), BF16),
                   jax.ShapeDtypeStruct((d // tn, m, 1), F32)],
        compiler_params=_params(("parallel", "parallel")),
        name="out_proj",
    )(oa, ob, w_bf, x2d, g_next.reshape(1, d))


def _up_kernel(a_ref, ssq_ref, w_ref, wf1_ref, wf2_ref, o_ref, wbf1_ref, wbf2_ref):
    _cast_chunk(wf1_ref, wbf1_ref)
    _cast_chunk(wf2_ref, wbf2_ref)
    k = a_ref.shape[1]
    for s in range(o_ref.shape[0] // OUT_SUB_ROWS):
        rs = slice(s * OUT_SUB_ROWS, (s + 1) * OUT_SUB_ROWS)
        ms = jnp.sum(ssq_ref[:, rs, :], axis=0) * (1.0 / k)
        inv_rms = lax.rsqrt(ms + NORM_EPS)
        acc = jnp.dot(a_ref[rs, :], w_ref[...], preferred_element_type=F32) * inv_rms
        o_ref[rs, :] = jnp.square(jnp.maximum(acc, 0.0)).astype(o_ref.dtype)


def _mlp_up(a, ssq, w_bf, w_next1, w_next2, tm=1024, tn=1024):
    m, k = a.shape
    n = w_bf.shape[1]
    grid = (n // tn, m // tm)
    in1, out1, shape1 = _cast_rider(w_next1, grid)
    in2, out2, shape2 = _cast_rider(w_next2, grid)
    return pl.pallas_call(
        _up_kernel,
        grid=grid,
        in_specs=[pl.BlockSpec((tm, k), lambda j, i: (i, 0)),
                  pl.BlockSpec((ssq.shape[0], tm, 1), lambda j, i: (0, i, 0)),
                  pl.BlockSpec((k, tn), lambda j, i: (0, j)),
                  in1, in2],
        out_specs=[pl.BlockSpec((tm, tn), lambda j, i: (i, j)), out1, out2],
        out_shape=[jax.ShapeDtypeStruct((m, n), BF16), shape1, shape2],
        compiler_params=_params(("parallel", "parallel")),
        name="mlp_up",
    )(a, ssq, w_bf, w_next1, w_next2)


DOWN_SUB_ROWS = 512


def _down_kernel(u_ref, w_ref, h_ref, o_ref, ob_ref):
    kk = pl.program_id(2)

    def accumulate(first):
        for s in range(o_ref.shape[0] // DOWN_SUB_ROWS):
            rs = slice(s * DOWN_SUB_ROWS, (s + 1) * DOWN_SUB_ROWS)
            part = jnp.dot(u_ref[rs, :], w_ref[...], preferred_element_type=F32)
            o_ref[rs, :] = (h_ref[rs, :] if first else o_ref[rs, :]) + part

    pl.when(kk == 0)(lambda: accumulate(True))
    pl.when(kk > 0)(lambda: accumulate(False))

    @pl.when(kk == pl.num_programs(2) - 1)
    def _():
        ob_ref[...] = o_ref[...].astype(ob_ref.dtype)


def _mlp_down(u, w_bf, h2d, tm=1024, tn=1024, tk=4096):
    m, k = u.shape
    d = w_bf.shape[1]
    return pl.pallas_call(
        _down_kernel,
        grid=(d // tn, m // tm, k // tk),
        in_specs=[pl.BlockSpec((tm, tk), lambda j, i, kk: (i, kk)),
                  pl.BlockSpec((tk, tn), lambda j, i, kk: (kk, j)),
                  pl.BlockSpec((tm, tn), lambda j, i, kk: (i, j))],
        out_specs=[pl.BlockSpec((tm, tn), lambda j, i, kk: (i, j)),
                   pl.BlockSpec((tm, tn), lambda j, i, kk: (i, j))],
        out_shape=[jax.ShapeDtypeStruct((m, d), F32),
                   jax.ShapeDtypeStruct((m, d), BF16)],
        compiler_params=_params(("parallel", "parallel", "arbitrary")),
        name="mlp_down",
    )(u, w_bf, h2d)


def _ple_kernel(p_ref, w_ref, g_ref, o_ref):
    y = jnp.dot(p_ref[...].astype(BF16), w_ref[...], preferred_element_type=F32)
    ms = jnp.mean(y * y, axis=-1, keepdims=True)
    o_ref[...] = y * lax.rsqrt(ms + NORM_EPS) * g_ref[...]


def _ple_embed(p2d, w_bf, g, tm=512):
    m, k = p2d.shape
    d = w_bf.shape[1]
    return pl.pallas_call(
        _ple_kernel,
        grid=(m // tm,),
        in_specs=[pl.BlockSpec((tm, k), lambda i: (i, 0)),
                  pl.BlockSpec((k, d), lambda i: (0, 0)),
                  pl.BlockSpec((1, d), lambda i: (0, 0))],
        out_specs=pl.BlockSpec((tm, d), lambda i: (i, 0)),
        out_shape=jax.ShapeDtypeStruct((m, d), F32),
        compiler_params=_params(("parallel",)),
        name="ple_embed",
    )(p2d, w_bf, g.reshape(1, d))


def _gate_kernel(hb_ref, w_ref, h_ref, pe_ref, o_ref):
    for s in range(o_ref.shape[0] // OUT_SUB_ROWS):
        rs = slice(s * OUT_SUB_ROWS, (s + 1) * OUT_SUB_ROWS)
        z = jnp.dot(hb_ref[rs, :], w_ref[...], preferred_element_type=F32)
        o_ref[rs, :] = h_ref[rs, :] + jax.nn.sigmoid(z) * pe_ref[rs, :]


def _ple_gate(hb, w_bf, h2d, pe, tm=512, tn=1024):
    m, d = h2d.shape
    tile = pl.BlockSpec((tm, tn), lambda j, i: (i, j))
    return pl.pallas_call(
        _gate_kernel,
        grid=(d // tn, m // tm),
        in_specs=[pl.BlockSpec((tm, d), lambda j, i: (i, 0)),
                  pl.BlockSpec((d, tn), lambda j, i: (0, j)),
                  tile, tile],
        out_specs=tile,
        out_shape=jax.ShapeDtypeStruct((m, d), F32),
        compiler_params=_params(("parallel", "parallel")),
        name="ple_gate",
    )(hb, w_bf, h2d, pe)


def _rope_tables(seq):
    pos = jnp.arange(seq, dtype=F32)

    def tab(dim):
        inv = 1.0 / (ROPE_THETA ** (jnp.arange(0, dim, 2, dtype=F32) / dim))
        ang = pos[:, None] * inv[None, :]
        ang = jnp.concatenate([ang, ang], axis=-1)
        return jnp.cos(ang), jnp.sin(ang)

    lane = jnp.arange(LANES)
    cos_a, sin_a = tab(HEAD_DIM)
    sin_a = jnp.where(lane < HEAD_DIM // 2, -sin_a, sin_a)
    cos_b, sin_b = tab(DIFF_QK_DIM)
    cos_b = jnp.concatenate([cos_b, cos_b], axis=-1)
    sin_b = jnp.concatenate([sin_b, sin_b], axis=-1)
    first_half = (lane % DIFF_QK_DIM) < DIFF_QK_DIM // 2
    sin_b_lo = jnp.where(first_half, -sin_b, 0.0)
    sin_b_hi = jnp.where(first_half, 0.0, sin_b)
    return cos_a, sin_a, cos_b, sin_b_lo, sin_b_hi


def _gain_rows(swa_q_g, swa_k_g, diff_q_g, diff_k_g):
    qa = jnp.tile(swa_q_g, SWA_Q_COLS // HEAD_DIM) * (LOG2E / math.sqrt(HEAD_DIM))
    ka = jnp.tile(swa_k_g, SWA_KV_COLS // HEAD_DIM)
    qb = jnp.tile(diff_q_g, DIFF_QK_COLS // DIFF_QK_DIM) * (LOG2E / math.sqrt(DIFF_QK_DIM))
    kb = jnp.tile(diff_k_g, DIFF_QK_COLS // DIFF_QK_DIM)
    cols = jnp.concatenate([qa, ka, jnp.ones((SWA_KV_COLS,), F32), qb, kb,
                            jnp.ones((DIFF_V_COLS,), F32)])
    return cols.reshape(IN_TILES, 1, IN_TN)


def _scores_bounded(q_g, k_g, dim, sinks=None):
    rounding_slack = 1.02
    bound = (math.sqrt(dim) * LOG2E * rounding_slack
             * jnp.max(jnp.abs(q_g)) * jnp.max(jnp.abs(k_g)))
    if sinks is not None:
        bound = jnp.maximum(bound, LOG2E * jnp.max(jnp.abs(sinks)))
    return (bound <= SCORE_BOUND).astype(jnp.int32).reshape(1)


def kernel(x, p, attn_norm_g, w_in, swa_q_norm_g, swa_k_norm_g, swa_sinks, diff_q_norm_g, diff_k_norm_g, diff_lambda_q1, diff_lambda_k1, diff_lambda_q2, diff_lambda_k2, diff_subln_g, w_o, mlp_norm_g, w_up, w_down, w_ple_proj, ple_norm_g, w_ple_gate):
    batch, seq, d = x.shape
    depth = w_in.shape[0]
    tables = _rope_tables(seq)
    h = x.reshape(batch * seq, d)
    for i in range(depth):
        lambda_init = 0.8 - 0.6 * math.exp(-0.3 * i)
        a = _rmsnorm_rows(h, attn_norm_g[i])
        gains = _gain_rows(swa_q_norm_g[i], swa_k_norm_g[i], diff_q_norm_g[i], diff_k_norm_g[i])
        proj, w_o_bf = _in_proj(a, w_in[i], gains, tables, w_o[i], seq)
        bounded_a = _scores_bounded(swa_q_norm_g[i], swa_k_norm_g[i], HEAD_DIM, swa_sinks[i])
        out_a = _swa_attention(proj, bounded_a, swa_sinks[i], batch, seq)
        bounded_b = _scores_bounded(diff_q_norm_g[i], diff_k_norm_g[i], DIFF_QK_DIM)
        out_b, w_up_bf = _diff_attention(proj, bounded_b, diff_lambda_q1[i], diff_lambda_k1[i],
                                         diff_lambda_q2[i], diff_lambda_k2[i], diff_subln_g[i],
                                         w_up[i], batch, seq, lambda_init)
        h, hg, ssq = _out_proj(out_a, out_b, w_o_bf, h, mlp_norm_g[i])
        u, w_down_bf, w_gate_bf = _mlp_up(hg, ssq, w_up_bf, w_down[i], w_ple_gate[i])
        h, hb = _mlp_down(u, w_down_bf, h)
        pe = _ple_embed(p[i].reshape(batch * seq, PLE_DIM), w_ple_proj[i].astype(BF16), ple_norm_g[i])
        h = _ple_gate(hb, w_gate_bf, h, pe)
    return h.reshape(batch, seq, d)
```

```python
import functools
import math

import jax
import jax.numpy as jnp
from jax import lax
from jax.experimental import pallas as pl
from jax.experimental.pallas import tpu as pltpu

F32 = jnp.float32
BF16 = jnp.bfloat16

D_MODEL = 4096
HEAD_DIM = 128
SWA_HEADS = 16
SWA_KV_HEADS = 4
SWA_GROUP = SWA_HEADS // SWA_KV_HEADS
BLOCK = 128
DIFF_HEADS = 16
DIFF_QK_DIM = 64
D_FF = 4 * D_MODEL
PLE_DIM = 256
ROPE_THETA = 10000.0
NORM_EPS = 1e-6
NEG_INF = -1e30
LOG2E = math.log2(math.e)
SCORE_BOUND = 40.0

SWA_Q_COLS = SWA_HEADS * HEAD_DIM
SWA_KV_COLS = SWA_KV_HEADS * HEAD_DIM
DIFF_QK_COLS = 2 * DIFF_HEADS * DIFF_QK_DIM
DIFF_V_COLS = DIFF_HEADS * HEAD_DIM
IN_COLS = SWA_Q_COLS + 2 * SWA_KV_COLS + 2 * DIFF_QK_COLS + DIFF_V_COLS
MIX_HALF = SWA_Q_COLS

LANES = 128
VMEM_LIMIT = 56 * 1024 * 1024

KA_BLK = SWA_Q_COLS // LANES
VA_BLK = KA_BLK + SWA_KV_COLS // LANES
QB_BLK = VA_BLK + SWA_KV_COLS // LANES
KB_BLK = QB_BLK + DIFF_QK_COLS // LANES
VB_BLK = KB_BLK + DIFF_QK_COLS // LANES

IN_TN = 1024
IN_TILES = IN_COLS // IN_TN
IN_HALF = IN_TN // 2
IN_SUB_ROWS = 256
IN_RIDER_ROUNDS = IN_TILES - 1
_HALF_KINDS = (['a'] * ((SWA_Q_COLS + SWA_KV_COLS) // IN_HALF) + ['p'] * (SWA_KV_COLS // IN_HALF)
               + ['b'] * (2 * DIFF_QK_COLS // IN_HALF) + ['p'] * (DIFF_V_COLS // IN_HALF))
_TILE_KINDS = [(_HALF_KINDS[2 * t], _HALF_KINDS[2 * t + 1]) for t in range(IN_TILES)]


def _params(sem):
    return pltpu.CompilerParams(dimension_semantics=sem, vmem_limit_bytes=VMEM_LIMIT)


def _cast_rider(w, grid):
    steps = math.prod(grid)
    rows = w.shape[0] // steps
    assert rows * steps == w.shape[0] and rows % 16 == 0, (w.shape, grid)

    def index_map(*idx):
        step = idx[0]
        for size, i in zip(grid[1:], idx[1:]):
            step = step * size + i
        return (step, 0)

    spec = pl.BlockSpec((rows, w.shape[1]), index_map)
    return spec, spec, jax.ShapeDtypeStruct(w.shape, BF16)


def _cast_chunk(wf_ref, wbf_ref):
    wbf_ref[...] = wf_ref[...].astype(BF16)


def _rmsnorm_kernel(x_ref, g_ref, o_ref):
    x = x_ref[...]
    ms = jnp.mean(x * x, axis=-1, keepdims=True)
    o_ref[...] = (x * lax.rsqrt(ms + NORM_EPS) * g_ref[...]).astype(o_ref.dtype)


def _rmsnorm_rows(x2d, g, tm=512):
    m, d = x2d.shape
    return pl.pallas_call(
        _rmsnorm_kernel,
        grid=(m // tm,),
        in_specs=[pl.BlockSpec((tm, d), lambda i: (i, 0)),
                  pl.BlockSpec((1, d), lambda i: (0, 0))],
        out_specs=pl.BlockSpec((tm, d), lambda i: (i, 0)),
        out_shape=jax.ShapeDtypeStruct((m, d), BF16),
        compiler_params=_params(("parallel",)),
        name="rmsnorm_rows",
    )(x2d, g.reshape(1, d))


def _in_proj_kernel(a_ref, wf_ref, g_ref, ca_ref, sa_ref, cb_ref, sbl_ref, sbh_ref, nextf_ref,
                    o_ref, nextbf_ref, w_ref):
    r = pl.program_id(0)
    i = pl.program_id(1)
    tm = a_ref.shape[0]
    chunk = wf_ref.shape[0]

    @pl.when((r >= 1) & (r <= IN_RIDER_ROUNDS))
    def _():
        _cast_chunk(nextf_ref, nextbf_ref)

    @pl.when(r < IN_TILES)
    def _():
        w_ref[r % 2, pl.ds(pl.multiple_of(i * chunk, chunk), chunk), :] = wf_ref[...].astype(BF16)

    def plain(acc, rs, c0):
        o_ref[rs, c0:c0 + IN_HALF] = acc.astype(o_ref.dtype)

    def heads128(acc, rs, c0):
        for c in range(IN_HALF // LANES):
            sl = slice(c * LANES, (c + 1) * LANES)
            osl = slice(c0 + c * LANES, c0 + (c + 1) * LANES)
            y = acc[:, sl]
            ms = jnp.mean(y * y, axis=-1, keepdims=True)
            yn = y * lax.rsqrt(ms + NORM_EPS) * g_ref[0][:, osl]
            out = yn * ca_ref[rs, :] + pltpu.roll(yn, HEAD_DIM // 2, 1) * sa_ref[rs, :]
            o_ref[rs, osl] = out.astype(o_ref.dtype)

    def heads64(acc, rs, c0):
        lane = lax.broadcasted_iota(jnp.int32, (1, LANES), 1)
        lo = lane < DIFF_QK_DIM
        for c in range(IN_HALF // LANES):
            sl = slice(c * LANES, (c + 1) * LANES)
            osl = slice(c0 + c * LANES, c0 + (c + 1) * LANES)
            y = acc[:, sl]
            sq = y * y
            s_lo = jnp.sum(jnp.where(lo, sq, 0.0), axis=-1, keepdims=True)
            s_hi = jnp.sum(jnp.where(lo, 0.0, sq), axis=-1, keepdims=True)
            ms = jnp.where(lo, s_lo, s_hi) * (1.0 / DIFF_QK_DIM)
            yn = y * lax.rsqrt(ms + NORM_EPS) * g_ref[0][:, osl]
            out = (yn * cb_ref[rs, :]
                   + pltpu.roll(yn, LANES - DIFF_QK_DIM // 2, 1) * sbl_ref[rs, :]
                   + pltpu.roll(yn, DIFF_QK_DIM // 2, 1) * sbh_ref[rs, :])
            o_ref[rs, osl] = out.astype(o_ref.dtype)

    epilogues = {'a': heads128, 'b': heads64, 'p': plain}

    def run(kinds):
        slot = (r - 1) % 2
        for s in range(tm // IN_SUB_ROWS):
            rs = slice(s * IN_SUB_ROWS, (s + 1) * IN_SUB_ROWS)
            acc = jnp.dot(a_ref[rs, :], w_ref[slot], preferred_element_type=F32)
            for half, kind in enumerate(kinds):
                c0 = half * IN_HALF
                epilogues[kind](acc[:, c0:c0 + IN_HALF], rs, c0)

    for kinds in sorted(set(_TILE_KINDS)):
        tiles = [t for t in range(IN_TILES) if _TILE_KINDS[t] == kinds]
        lo_t, hi_t = tiles[0], tiles[-1]
        assert tiles == list(range(lo_t, hi_t + 1))
        pl.when((r - 1 >= lo_t) & (r - 1 <= hi_t))(functools.partial(run, kinds))


def _in_proj(a, w, gains, tables, w_next, seq, tm=1024):
    m, k = a.shape
    steps = m // tm
    pos_blocks = seq // tm
    last = IN_TILES - 1
    rider_chunks = IN_RIDER_ROUNDS * steps
    rider_rows = w_next.shape[0] // rider_chunks
    assert rider_rows * rider_chunks == w_next.shape[0] and rider_rows % 16 == 0
    row = lambda r, i: jnp.where(r == 0, 0, i)
    tile = lambda r, i: jnp.maximum(r - 1, 0)
    tab_spec = pl.BlockSpec((tm, LANES), lambda r, i: (i % pos_blocks, 0))
    rider_spec = pl.BlockSpec((rider_rows, w_next.shape[1]),
                              lambda r, i: (jnp.clip((r - 1) * steps + i, 0, rider_chunks - 1), 0))
    return pl.pallas_call(
        _in_proj_kernel,
        grid=(IN_TILES + 1, steps),
        in_specs=[pl.BlockSpec((tm, k), lambda r, i: (row(r, i), 0)),
                  pl.BlockSpec((k // steps, IN_TN), lambda r, i: (i, jnp.minimum(r, last))),
                  pl.BlockSpec((1, 1, IN_TN), lambda r, i: (tile(r, i), 0, 0)),
                  tab_spec, tab_spec, tab_spec, tab_spec, tab_spec, rider_spec],
        out_specs=[pl.BlockSpec((tm, IN_TN), lambda r, i: (row(r, i), tile(r, i))), rider_spec],
        out_shape=[jax.ShapeDtypeStruct((m, IN_COLS), BF16),
                   jax.ShapeDtypeStruct(w_next.shape, BF16)],
        scratch_shapes=[pltpu.VMEM((2, k, IN_TN), BF16)],
        compiler_params=_params(("arbitrary", "arbitrary")),
        name="in_proj",
    )(a, w, gains, *tables, w_next)


SWA_STEP_BLOCKS = 2


def _swa_kernel(bounded_ref, sink_ref, sinkrow_ref, q_ref, kp_ref, kc_ref, vp_ref, vc_ref, o_ref):
    j = pl.program_id(1)
    nt = (((1,), (1,)), ((), ()))
    rows = SWA_GROUP * BLOCK
    r = lax.broadcasted_iota(jnp.int32, (rows, 2 * BLOCK), 0) & (BLOCK - 1)
    c = lax.broadcasted_iota(jnp.int32, (rows, 2 * BLOCK), 1)
    window = (c > r) & (c <= r + BLOCK)
    valid = [window & ((j > 0) | (c >= BLOCK))] + [window] * (SWA_STEP_BLOCKS - 1)
    ones = jnp.ones((2 * BLOCK, HEAD_DIM), BF16)

    def masked_scores(h, t):
        ks = slice(h * HEAD_DIM, (h + 1) * HEAD_DIM)
        qr = slice(t * BLOCK, (t + 1) * BLOCK)
        if t == 0:
            kk = jnp.concatenate([kp_ref[:, ks], kc_ref[:BLOCK, ks]], axis=0)
            vv = jnp.concatenate([vp_ref[:, ks], vc_ref[:BLOCK, ks]], axis=0)
        else:
            kk = kc_ref[(t - 1) * BLOCK:(t + 1) * BLOCK, ks]
            vv = vc_ref[(t - 1) * BLOCK:(t + 1) * BLOCK, ks]
        heads = [h * SWA_GROUP + g for g in range(SWA_GROUP)]
        q4 = jnp.concatenate([q_ref[qr, hd * HEAD_DIM:(hd + 1) * HEAD_DIM] for hd in heads], axis=0)
        s = lax.dot_general(q4, kk, nt, preferred_element_type=F32)
        return jnp.where(valid[t], s, NEG_INF), vv, heads, qr

    @pl.when(bounded_ref[0] != 0)
    def _():
        for h in range(SWA_KV_HEADS):
            for t in range(SWA_STEP_BLOCKS):
                s, vv, heads, qr = masked_scores(h, t)
                vaug = jnp.concatenate([vv, ones], axis=1)
                o = jnp.dot(jnp.exp2(s).astype(BF16), vaug, preferred_element_type=F32)
                for g, hd in enumerate(heads):
                    rs = slice(g * BLOCK, (g + 1) * BLOCK)
                    sink = jnp.exp2(sinkrow_ref[hd:hd + 1, :] * LOG2E)
                    out = o[rs, :HEAD_DIM] / (o[rs, HEAD_DIM:] + sink)
                    o_ref[qr, hd * HEAD_DIM:(hd + 1) * HEAD_DIM] = out.astype(o_ref.dtype)

    @pl.when(bounded_ref[0] == 0)
    def _():
        for h in range(SWA_KV_HEADS):
            for t in range(SWA_STEP_BLOCKS):
                s, vv, heads, qr = masked_scores(h, t)
                for g, hd in enumerate(heads):
                    sg = s[g * BLOCK:(g + 1) * BLOCK]
                    sink = sink_ref[hd] * LOG2E
                    mx = jnp.maximum(jnp.max(sg, axis=-1, keepdims=True), sink)
                    p = jnp.exp2(sg - mx)
                    denom = jnp.sum(p, axis=-1, keepdims=True) + jnp.exp2(sink - mx)
                    o = jnp.dot(p.astype(BF16), vv, preferred_element_type=F32)
                    o_ref[qr, hd * HEAD_DIM:(hd + 1) * HEAD_DIM] = (o / denom).astype(o_ref.dtype)


def _swa_attention(proj, bounded, sinks, batch, seq):
    step_rows = SWA_STEP_BLOCKS * BLOCK
    ns = seq // step_rows
    qw = SWA_HEADS * HEAD_DIM
    kw = SWA_KV_HEADS * HEAD_DIM
    row = lambda b, j: b * ns + j
    prev = lambda b, j: (b * ns + j) * SWA_STEP_BLOCKS - jnp.minimum(j, 1)
    k_blk = SWA_Q_COLS // kw
    v_blk = k_blk + 1
    sink_rows = jnp.broadcast_to(sinks[:, None], (SWA_HEADS, HEAD_DIM))
    return pl.pallas_call(
        _swa_kernel,
        grid=(batch, ns),
        in_specs=[pl.BlockSpec(memory_space=pltpu.SMEM),
                  pl.BlockSpec(memory_space=pltpu.SMEM),
                  pl.BlockSpec((SWA_HEADS, HEAD_DIM), lambda b, j: (0, 0)),
                  pl.BlockSpec((step_rows, qw), lambda b, j: (row(b, j), 0)),
                  pl.BlockSpec((BLOCK, kw), lambda b, j: (prev(b, j), k_blk)),
                  pl.BlockSpec((step_rows, kw), lambda b, j: (row(b, j), k_blk)),
                  pl.BlockSpec((BLOCK, kw), lambda b, j: (prev(b, j), v_blk)),
                  pl.BlockSpec((step_rows, kw), lambda b, j: (row(b, j), v_blk))],
        out_specs=pl.BlockSpec((step_rows, qw), lambda b, j: (row(b, j), 0)),
        out_shape=jax.ShapeDtypeStruct((batch * seq, MIX_HALF), BF16),
        compiler_params=_params(("parallel", "parallel")),
        name="swa_attention",
    )(bounded, sinks, sink_rows, proj, proj, proj, proj, proj)


DIFF_TQ = 2048
DIFF_TK = 256
DIFF_DIAG = DIFF_TQ // DIFF_TK
assert DIFF_DIAG * DIFF_TK == DIFF_TQ and DIFF_DIAG % 2 == 0

def _diff_kernel(bounded_ref, lq1_ref, lk1_ref, lq2_ref, lk2_ref, subg_ref,
                 q1_ref, q2_ref, k1_ref, k2_ref, v_ref, wf_ref, o_ref, wbf_ref,
                 vaug_ref, acc_ref, *, lambda_init):
    _cast_chunk(wf_ref, wbf_ref)
    qi = pl.program_id(2)
    tq = DIFF_TQ
    nt = (((1,), (1,)), ((), ()))
    lane = lax.broadcasted_iota(jnp.int32, (tq, LANES), 1)
    qs = []
    for q_ref in (q1_ref, q2_ref):
        q = q_ref[...]
        for e in range(2):
            keep = (lane >= e * DIFF_QK_DIM) & (lane < (e + 1) * DIFF_QK_DIM)
            qs.append(jnp.where(keep, q, jnp.zeros_like(q)))

    tk = DIFF_TK
    k_refs = (k1_ref, k2_ref)
    row = lax.broadcasted_iota(jnp.int32, (tq, tk), 0)
    col = lax.broadcasted_iota(jnp.int32, (tq, tk), 1)
    causal = col <= row

    def scores(slot, j, r0, masked):
        off = pl.multiple_of(j * tk, tk)
        s = lax.dot_general(qs[slot][r0:], k_refs[slot // 2][pl.ds(off, tk), :], nt,
                            preferred_element_type=F32)
        if masked:
            s = jnp.where(causal[:tq - r0], s, NEG_INF)
        return s

    lam = (jnp.exp(jnp.sum(lq1_ref[...] * lk1_ref[...], axis=-1, keepdims=True))
           - jnp.exp(jnp.sum(lq2_ref[...] * lk2_ref[...], axis=-1, keepdims=True))
           + lambda_init)

    def finalize(e, o1, o2):
        o = o1 - lam * o2
        ms = jnp.mean(o * o, axis=-1, keepdims=True)
        on = o * lax.rsqrt(ms + NORM_EPS) * subg_ref[...]
        o_ref[:, e * HEAD_DIM:(e + 1) * HEAD_DIM] = (on * (1.0 - lambda_init)).astype(o_ref.dtype)

    @pl.when(qi == 0)
    def _():
        for e in range(2):
            vaug_ref[e, :, :HEAD_DIM] = v_ref[:, e * HEAD_DIM:(e + 1) * HEAD_DIM]
            vaug_ref[e, :, HEAD_DIM:] = jnp.ones((v_ref.shape[0], HEAD_DIM), BF16)

    @pl.when(bounded_ref[0] != 0)
    def _():
        def step(j, r0, masked, assign=False):
            off = pl.multiple_of(j * tk, tk)
            ps = [jnp.exp2(scores(slot, j, r0, masked)).astype(BF16) for slot in range(4)]
            for e in range(2):
                pp = jnp.concatenate([ps[e], ps[2 + e]], axis=0)
                pv = jnp.dot(pp, vaug_ref[e, pl.ds(off, tk), :], preferred_element_type=F32)
                if assign:
                    acc_ref[e] = pv
                else:
                    acc_ref[e, r0:tq] += pv[:tq - r0]
                    acc_ref[e, tq + r0:] += pv[tq - r0:]

        step(DIFF_DIAG * qi, 0, True, assign=True)
        for d in range(1, DIFF_DIAG):
            step(DIFF_DIAG * qi + d, d * tk, True)

        def body(jj, carry):
            step(2 * jj, 0, False)
            step(2 * jj + 1, 0, False)
            return carry

        lax.fori_loop(0, (DIFF_DIAG * qi) // 2, body, 0)
        for e in range(2):
            a = acc_ref[e]
            finalize(e, a[:tq, :HEAD_DIM] / a[:tq, HEAD_DIM:],
                     a[tq:, :HEAD_DIM] / a[tq:, HEAD_DIM:])

    @pl.when(bounded_ref[0] == 0)
    def _():
        M, L = HEAD_DIM, HEAD_DIM + 1
        acc_ref[:, :, :HEAD_DIM] = jnp.zeros((2, 2 * tq, HEAD_DIM), F32)
        acc_ref[:, :, M:M + 1] = jnp.full((2, 2 * tq, 1), NEG_INF, F32)
        acc_ref[:, :, L:L + 1] = jnp.zeros((2, 2 * tq, 1), F32)

        def step(j, r0, masked):
            off = pl.multiple_of(j * tk, tk)
            vb = v_ref[pl.ds(off, tk), :]
            for slot in range(4):
                mp, e = slot // 2, slot % 2
                rows = slice(mp * tq + r0, (mp + 1) * tq)
                s = scores(slot, j, r0, masked)
                m_prev = acc_ref[e, rows, M:M + 1]
                m_new = jnp.maximum(m_prev, jnp.max(s, axis=-1, keepdims=True))
                alpha = jnp.exp2(m_prev - m_new)
                p = jnp.exp2(s - m_new)
                acc_ref[e, rows, L:L + 1] = (alpha * acc_ref[e, rows, L:L + 1]
                                             + jnp.sum(p, axis=-1, keepdims=True))
                pv = jnp.dot(p.astype(BF16), vb[:, e * HEAD_DIM:(e + 1) * HEAD_DIM],
                             preferred_element_type=F32)
                acc_ref[e, rows, :HEAD_DIM] = alpha * acc_ref[e, rows, :HEAD_DIM] + pv
                acc_ref[e, rows, M:M + 1] = m_new

        def body(j, carry):
            step(j, 0, False)
            return carry

        lax.fori_loop(0, DIFF_DIAG * qi, body, 0)
        for d in range(DIFF_DIAG):
            step(DIFF_DIAG * qi + d, d * tk, True)
        for e in range(2):
            a = acc_ref[e]
            finalize(e, a[:tq, :HEAD_DIM] / a[:tq, L:L + 1], a[tq:, :HEAD_DIM] / a[tq:, L:L + 1])


def _diff_attention(proj, bounded, lq1, lk1, lq2, lk2, subg, w_next, batch, seq, lambda_init):
    tq = DIFF_TQ
    nq = seq // tq
    pairs = DIFF_HEADS // 2
    w_in_spec, w_out_spec, w_out_shape = _cast_rider(w_next, (batch, pairs, nq))
    map_blks = DIFF_HEADS * DIFF_QK_DIM // LANES
    vec = lambda d: pl.BlockSpec((1, d), lambda b, h, i: (0, 0))
    return pl.pallas_call(
        functools.partial(_diff_kernel, lambda_init=lambda_init),
        grid=(batch, pairs, nq),
        in_specs=[pl.BlockSpec(memory_space=pltpu.SMEM),
                  vec(DIFF_QK_DIM), vec(DIFF_QK_DIM), vec(DIFF_QK_DIM), vec(DIFF_QK_DIM),
                  vec(HEAD_DIM),
                  pl.BlockSpec((tq, LANES), lambda b, h, i: (b * nq + i, QB_BLK + h)),
                  pl.BlockSpec((tq, LANES), lambda b, h, i: (b * nq + i, QB_BLK + map_blks + h)),
                  pl.BlockSpec((seq, LANES), lambda b, h, i: (b, KB_BLK + h)),
                  pl.BlockSpec((seq, LANES), lambda b, h, i: (b, KB_BLK + map_blks + h)),
                  pl.BlockSpec((seq, 2 * HEAD_DIM), lambda b, h, i: (b, VB_BLK // 2 + h)),
                  w_in_spec],
        out_specs=[pl.BlockSpec((tq, 2 * HEAD_DIM), lambda b, h, i: (b * nq + i, h)), w_out_spec],
        out_shape=[jax.ShapeDtypeStruct((batch * seq, MIX_HALF), BF16), w_out_shape],
        scratch_shapes=[pltpu.VMEM((2, seq, 2 * HEAD_DIM), BF16),
                        pltpu.VMEM((2, 2 * tq, 2 * HEAD_DIM), F32)],
        compiler_params=_params(("parallel", "parallel", "arbitrary")),
        name="diff_attention",
    )(bounded, lq1.reshape(1, -1), lk1.reshape(1, -1), lq2.reshape(1, -1), lk2.reshape(1, -1),
      subg.reshape(1, -1), proj, proj, proj, proj, proj, w_next)


OUT_SUB_ROWS = 512


def _out_proj_kernel(oa_ref, ob_ref, w_ref, x_ref, g_ref, o_ref, hg_ref, ssq_ref):
    for s in range(o_ref.shape[0] // OUT_SUB_ROWS):
        rs = slice(s * OUT_SUB_ROWS, (s + 1) * OUT_SUB_ROWS)
        acc = jnp.dot(oa_ref[rs, :], w_ref[:MIX_HALF, :], preferred_element_type=F32)
        acc += jnp.dot(ob_ref[rs, :], w_ref[MIX_HALF:, :], preferred_element_type=F32)
        h = x_ref[rs, :] + acc
        o_ref[rs, :] = h
        hg_ref[rs, :] = (h * g_ref[...]).astype(hg_ref.dtype)
        ssq_ref[0, rs, :] = jnp.sum(h * h, axis=-1, keepdims=True)


def _out_proj(oa, ob, w_bf, x2d, g_next, tm=1024, tn=1024):
    m, d = x2d.shape
    tile = pl.BlockSpec((tm, tn), lambda n, i: (i, n))
    return pl.pallas_call(
        _out_proj_kernel,
        grid=(d // tn, m // tm),
        in_specs=[pl.BlockSpec((tm, MIX_HALF), lambda n, i: (i, 0)),
                  pl.BlockSpec((tm, MIX_HALF), lambda n, i: (i, 0)),
                  pl.BlockSpec((2 * MIX_HALF, tn), lambda n, i: (0, n)),
                  tile,
                  pl.BlockSpec((1, tn), lambda n, i: (0, n))],
        out_specs=[tile, tile, pl.BlockSpec((1, tm, 1), lambda n, i: (n, i, 0))],
        out_shape=[jax.ShapeDtypeStruct((m, d), F32),
                   jax.ShapeDtypeStruct((m, d), BF16),
                   jax.ShapeDtypeStruct((d // tn, m, 1), F32)],
        compiler_params=_params(("parallel", "parallel")),
        name="out_proj",
    )(oa, ob, w_bf, x2d, g_next.reshape(1, d))


def _up_kernel(a_ref, ssq_ref, w_ref, wf1_ref, wf2_ref, o_ref, wbf1_ref, wbf2_ref):
    _cast_chunk(wf1_ref, wbf1_ref)
    _cast_chunk(wf2_ref, wbf2_ref)
    k = a_ref.shape[1]
    for s in range(o_ref.shape[0] // OUT_SUB_ROWS):
        rs = slice(s * OUT_SUB_ROWS, (s + 1) * OUT_SUB_ROWS)
        ms = jnp.sum(ssq_ref[:, rs, :], axis=0) * (1.0 / k)
        inv_rms = lax.rsqrt(ms + NORM_EPS)
        acc = jnp.dot(a_ref[rs, :], w_ref[...], preferred_element_type=F32) * inv_rms
        o_ref[rs, :] = jnp.square(jnp.maximum(acc, 0.0)).astype(o_ref.dtype)


def _mlp_up(a, ssq, w_bf, w_next1, w_next2, tm=1024, tn=1024):
    m, k = a.shape
    n = w_bf.shape[1]
    grid = (n // tn, m // tm)
    in1, out1, shape1 = _cast_rider(w_next1, grid)
    in2, out2, shape2 = _cast_rider(w_next2, grid)
    return pl.pallas_call(
        _up_kernel,
        grid=grid,
        in_specs=[pl.BlockSpec((tm, k), lambda j, i: (i, 0)),
                  pl.BlockSpec((ssq.shape[0], tm, 1), lambda j, i: (0, i, 0)),
                  pl.BlockSpec((k, tn), lambda j, i: (0, j)),
                  in1, in2],
        out_specs=[pl.BlockSpec((tm, tn), lambda j, i: (i, j)), out1, out2],
        out_shape=[jax.ShapeDtypeStruct((m, n), BF16), shape1, shape2],
        compiler_params=_params(("parallel", "parallel")),
        name="mlp_up",
    )(a, ssq, w_bf, w_next1, w_next2)


DOWN_SUB_ROWS = 512


def _down_kernel(u_ref, w_ref, h_ref, o_ref, ob_ref):
    kk = pl.program_id(2)

    def accumulate(first):
        for s in range(o_ref.shape[0] // DOWN_SUB_ROWS):
            rs = slice(s * DOWN_SUB_ROWS, (s + 1) * DOWN_SUB_ROWS)
            part = jnp.dot(u_ref[rs, :], w_ref[...], preferred_element_type=F32)
            o_ref[rs, :] = (h_ref[rs, :] if first else o_ref[rs, :]) + part

    pl.when(kk == 0)(lambda: accumulate(True))
    pl.when(kk > 0)(lambda: accumulate(False))

    @pl.when(kk == pl.num_programs(2) - 1)
    def _():
        ob_ref[...] = o_ref[...].astype(ob_ref.dtype)


def _mlp_down(u, w_bf, h2d, tm=1024, tn=1024, tk=4096):
    m, k = u.shape
    d = w_bf.shape[1]
    return pl.pallas_call(
        _down_kernel,
        grid=(d // tn, m // tm, k // tk),
        in_specs=[pl.BlockSpec((tm, tk), lambda j, i, kk: (i, kk)),
                  pl.BlockSpec((tk, tn), lambda j, i, kk: (kk, j)),
                  pl.BlockSpec((tm, tn), lambda j, i, kk: (i, j))],
        out_specs=[pl.BlockSpec((tm, tn), lambda j, i, kk: (i, j)),
                   pl.BlockSpec((tm, tn), lambda j, i, kk: (i, j))],
        out_shape=[jax.ShapeDtypeStruct((m, d), F32),
                   jax.ShapeDtypeStruct((m, d), BF16)],
        compiler_params=_params(("parallel", "parallel", "arbitrary")),
        name="mlp_down",
    )(u, w_bf, h2d)


def _ple_kernel(p_ref, w_ref, g_ref, o_ref):
    y = jnp.dot(p_ref[...].astype(BF16), w_ref[...], preferred_element_type=F32)
    ms = jnp.mean(y * y, axis=-1, keepdims=True)
    o_ref[...] = y * lax.rsqrt(ms + NORM_EPS) * g_ref[...]


def _ple_embed(p2d, w_bf, g, tm=512):
    m, k = p2d.shape
    d = w_bf.shape[1]
    return pl.pallas_call(
        _ple_kernel,
        grid=(m // tm,),
        in_specs=[pl.BlockSpec((tm, k), lambda i: (i, 0)),
                  pl.BlockSpec((k, d), lambda i: (0, 0)),
                  pl.BlockSpec((1, d), lambda i: (0, 0))],
        out_specs=pl.BlockSpec((tm, d), lambda i: (i, 0)),
        out_shape=jax.ShapeDtypeStruct((m, d), F32),
        compiler_params=_params(("parallel",)),
        name="ple_embed",
    )(p2d, w_bf, g.reshape(1, d))


def _gate_kernel(hb_ref, w_ref, h_ref, pe_ref, o_ref):
    for s in range(o_ref.shape[0] // OUT_SUB_ROWS):
        rs = slice(s * OUT_SUB_ROWS, (s + 1) * OUT_SUB_ROWS)
        z = jnp.dot(hb_ref[rs, :], w_ref[...], preferred_element_type=F32)
        o_ref[rs, :] = h_ref[rs, :] + jax.nn.sigmoid(z) * pe_ref[rs, :]


def _ple_gate(hb, w_bf, h2d, pe, tm=512, tn=1024):
    m, d = h2d.shape
    tile = pl.BlockSpec((tm, tn), lambda j, i: (i, j))
    return pl.pallas_call(
        _gate_kernel,
        grid=(d // tn, m // tm),
        in_specs=[pl.BlockSpec((tm, d), lambda j, i: (i, 0)),
                  pl.BlockSpec((d, tn), lambda j, i: (0, j)),
                  tile, tile],
        out_specs=tile,
        out_shape=jax.ShapeDtypeStruct((m, d), F32),
        compiler_params=_params(("parallel", "parallel")),
        name="ple_gate",
    )(hb, w_bf, h2d, pe)


def _rope_tables(seq):
    pos = jnp.arange(seq, dtype=F32)

    def tab(dim):
        inv = 1.0 / (ROPE_THETA ** (jnp.arange(0, dim, 2, dtype=F32) / dim))
        ang = pos[:, None] * inv[None, :]
        ang = jnp.concatenate([ang, ang], axis=-1)
        return jnp.cos(ang), jnp.sin(ang)

    lane = jnp.arange(LANES)
    cos_a, sin_a = tab(HEAD_DIM)
    sin_a = jnp.where(lane < HEAD_DIM // 2, -sin_a, sin_a)
    cos_b, sin_b = tab(DIFF_QK_DIM)
    cos_b = jnp.concatenate([cos_b, cos_b], axis=-1)
    sin_b = jnp.concatenate([sin_b, sin_b], axis=-1)
    first_half = (lane % DIFF_QK_DIM) < DIFF_QK_DIM // 2
    sin_b_lo = jnp.where(first_half, -sin_b, 0.0)
    sin_b_hi = jnp.where(first_half, 0.0, sin_b)
    return cos_a, sin_a, cos_b, sin_b_lo, sin_b_hi


def _gain_rows(swa_q_g, swa_k_g, diff_q_g, diff_k_g):
    qa = jnp.tile(swa_q_g, SWA_Q_COLS // HEAD_DIM) * (LOG2E / math.sqrt(HEAD_DIM))
    ka = jnp.tile(swa_k_g, SWA_KV_COLS // HEAD_DIM)
    qb = jnp.tile(diff_q_g, DIFF_QK_COLS // DIFF_QK_DIM) * (LOG2E / math.sqrt(DIFF_QK_DIM))
    kb = jnp.tile(diff_k_g, DIFF_QK_COLS // DIFF_QK_DIM)
    cols = jnp.concatenate([qa, ka, jnp.ones((SWA_KV_COLS,), F32), qb, kb,
                            jnp.ones((DIFF_V_COLS,), F32)])
    return cols.reshape(IN_TILES, 1, IN_TN)


def _scores_bounded(q_g, k_g, dim, sinks=None):
    rounding_slack = 1.02
    bound = (math.sqrt(dim) * LOG2E * rounding_slack
             * jnp.max(jnp.abs(q_g)) * jnp.max(jnp.abs(k_g)))
    if sinks is not None:
        bound = jnp.maximum(bound, LOG2E * jnp.max(jnp.abs(sinks)))
    return (bound <= SCORE_BOUND).astype(jnp.int32).reshape(1)


def kernel(x, p, attn_norm_g, w_in, swa_q_norm_g, swa_k_norm_g, swa_sinks, diff_q_norm_g, diff_k_norm_g, diff_lambda_q1, diff_lambda_k1, diff_lambda_q2, diff_lambda_k2, diff_subln_g, w_o, mlp_norm_g, w_up, w_down, w_ple_proj, ple_norm_g, w_ple_gate):
    batch, seq, d = x.shape
    depth = w_in.shape[0]
    tables = _rope_tables(seq)
    h = x.reshape(batch * seq, d)
    for i in range(depth):
        lambda_init = 0.8 - 0.6 * math.exp(-0.3 * i)
        a = _rmsnorm_rows(h, attn_norm_g[i])
        gains = _gain_rows(swa_q_norm_g[i], swa_k_norm_g[i], diff_q_norm_g[i], diff_k_norm_g[i])
        proj, w_o_bf = _in_proj(a, w_in[i], gains, tables, w_o[i], seq)
        bounded_a = _scores_bounded(swa_q_norm_g[i], swa_k_norm_g[i], HEAD_DIM, swa_sinks[i])
        out_a = _swa_attention(proj, bounded_a, swa_sinks[i], batch, seq)
        bounded_b = _scores_bounded(diff_q_norm_g[i], diff_k_norm_g[i], DIFF_QK_DIM)
        out_b, w_up_bf = _diff_attention(proj, bounded_b, diff_lambda_q1[i], diff_lambda_k1[i],
                                         diff_lambda_q2[i], diff_lambda_k2[i], diff_subln_g[i],
                                         w_up[i], batch, seq, lambda_init)
        h, hg, ssq = _out_proj(out_a, out_b, w_o_bf, h, mlp_norm_g[i])
        u, w_down_bf, w_gate_bf = _mlp_up(hg, ssq, w_up_bf, w_down[i], w_ple_gate[i])
        h, hb = _mlp_down(u, w_down_bf, h)
        pe = _ple_embed(p[i].reshape(batch * seq, PLE_DIM), w_ple_proj[i].astype(BF16), ple_norm_g[i])
        h = _ple_gate(hb, w_gate_bf, h, pe)
    return h.reshape(batch, seq, d)
```

```python
import functools
import math

import jax
import jax.numpy as jnp
from jax import lax
from jax.experimental import pallas as pl
from jax.experimental.pallas import tpu as pltpu

F32 = jnp.float32
BF16 = jnp.bfloat16

D_MODEL = 4096
HEAD_DIM = 128
SWA_HEADS = 16
SWA_KV_HEADS = 4
SWA_GROUP = SWA_HEADS // SWA_KV_HEADS
BLOCK = 128
DIFF_HEADS = 16
DIFF_QK_DIM = 64
D_FF = 4 * D_MODEL
PLE_DIM = 256
ROPE_THETA = 10000.0
NORM_EPS = 1e-6
NEG_INF = -1e30
LOG2E = math.log2(math.e)
SCORE_BOUND = 40.0

SWA_Q_COLS = SWA_HEADS * HEAD_DIM
SWA_KV_COLS = SWA_KV_HEADS * HEAD_DIM
DIFF_QK_COLS = 2 * DIFF_HEADS * DIFF_QK_DIM
DIFF_V_COLS = DIFF_HEADS * HEAD_DIM
IN_COLS = SWA_Q_COLS + 2 * SWA_KV_COLS + 2 * DIFF_QK_COLS + DIFF_V_COLS
MIX_HALF = SWA_Q_COLS

LANES = 128
VMEM_LIMIT = 56 * 1024 * 1024

KA_BLK = SWA_Q_COLS // LANES
VA_BLK = KA_BLK + SWA_KV_COLS // LANES
QB_BLK = VA_BLK + SWA_KV_COLS // LANES
KB_BLK = QB_BLK + DIFF_QK_COLS // LANES
VB_BLK = KB_BLK + DIFF_QK_COLS // LANES

IN_TN = 1024
IN_TILES = IN_COLS // IN_TN
IN_HALF = IN_TN // 2
IN_SUB_ROWS = 256
IN_RIDER_ROUNDS = IN_TILES - 1
_HALF_KINDS = (['a'] * ((SWA_Q_COLS + SWA_KV_COLS) // IN_HALF) + ['p'] * (SWA_KV_COLS // IN_HALF)
               + ['b'] * (2 * DIFF_QK_COLS // IN_HALF) + ['p'] * (DIFF_V_COLS // IN_HALF))
_TILE_KINDS = [(_HALF_KINDS[2 * t], _HALF_KINDS[2 * t + 1]) for t in range(IN_TILES)]


def _params(sem):
    return pltpu.CompilerParams(dimension_semantics=sem, vmem_limit_bytes=VMEM_LIMIT)


def _cast_rider(w, grid):
    steps = math.prod(grid)
    rows = w.shape[0] // steps
    assert rows * steps == w.shape[0] and rows % 16 == 0, (w.shape, grid)

    def index_map(*idx):
        step = idx[0]
        for size, i in zip(grid[1:], idx[1:]):
            step = step * size + i
        return (step, 0)

    spec = pl.BlockSpec((rows, w.shape[1]), index_map)
    return spec, spec, jax.ShapeDtypeStruct(w.shape, BF16)


def _cast_chunk(wf_ref, wbf_ref):
    wbf_ref[...] = wf_ref[...].astype(BF16)


def _rmsnorm_kernel(x_ref, g_ref, o_ref):
    x = x_ref[...]
    ms = jnp.mean(x * x, axis=-1, keepdims=True)
    o_ref[...] = (x * lax.rsqrt(ms + NORM_EPS) * g_ref[...]).astype(o_ref.dtype)


def _rmsnorm_rows(x2d, g, tm=512):
    m, d = x2d.shape
    return pl.pallas_call(
        _rmsnorm_kernel,
        grid=(m // tm,),
        in_specs=[pl.BlockSpec((tm, d), lambda i: (i, 0)),
                  pl.BlockSpec((1, d), lambda i: (0, 0))],
        out_specs=pl.BlockSpec((tm, d), lambda i: (i, 0)),
        out_shape=jax.ShapeDtypeStruct((m, d), BF16),
        compiler_params=_params(("parallel",)),
        name="rmsnorm_rows",
    )(x2d, g.reshape(1, d))


def _in_proj_kernel(a_ref, wf_ref, g_ref, ca_ref, sa_ref, cb_ref, sbl_ref, sbh_ref, nextf_ref,
                    o_ref, nextbf_ref, w_ref):
    r = pl.program_id(0)
    i = pl.program_id(1)
    tm = a_ref.shape[0]
    chunk = wf_ref.shape[0]

    @pl.when((r >= 1) & (r <= IN_RIDER_ROUNDS))
    def _():
        _cast_chunk(nextf_ref, nextbf_ref)

    @pl.when(r < IN_TILES)
    def _():
        w_ref[r % 2, pl.ds(pl.multiple_of(i * chunk, chunk), chunk), :] = wf_ref[...].astype(BF16)

    def plain(acc, rs, c0):
        o_ref[rs, c0:c0 + IN_HALF] = acc.astype(o_ref.dtype)

    def heads128(acc, rs, c0):
        for c in range(IN_HALF // LANES):
            sl = slice(c * LANES, (c + 1) * LANES)
            osl = slice(c0 + c * LANES, c0 + (c + 1) * LANES)
            y = acc[:, sl]
            ms = jnp.mean(y * y, axis=-1, keepdims=True)
            yn = y * lax.rsqrt(ms + NORM_EPS) * g_ref[0][:, osl]
            out = yn * ca_ref[rs, :] + pltpu.roll(yn, HEAD_DIM // 2, 1) * sa_ref[rs, :]
            o_ref[rs, osl] = out.astype(o_ref.dtype)

    def heads64(acc, rs, c0):
        lane = lax.broadcasted_iota(jnp.int32, (1, LANES), 1)
        lo = lane < DIFF_QK_DIM
        for c in range(IN_HALF // LANES):
            sl = slice(c * LANES, (c + 1) * LANES)
            osl = slice(c0 + c * LANES, c0 + (c + 1) * LANES)
            y = acc[:, sl]
            sq = y * y
            s_lo = jnp.sum(jnp.where(lo, sq, 0.0), axis=-1, keepdims=True)
            s_hi = jnp.sum(jnp.where(lo, 0.0, sq), axis=-1, keepdims=True)
            ms = jnp.where(lo, s_lo, s_hi) * (1.0 / DIFF_QK_DIM)
            yn = y * lax.rsqrt(ms + NORM_EPS) * g_ref[0][:, osl]
            out = (yn * cb_ref[rs, :]
                   + pltpu.roll(yn, LANES - DIFF_QK_DIM // 2, 1) * sbl_ref[rs, :]
                   + pltpu.roll(yn, DIFF_QK_DIM // 2, 1) * sbh_ref[rs, :])
            o_ref[rs, osl] = out.astype(o_ref.dtype)

    epilogues = {'a': heads128, 'b': heads64, 'p': plain}

    def run(kinds):
        slot = (r - 1) % 2
        for s in range(tm // IN_SUB_ROWS):
            rs = slice(s * IN_SUB_ROWS, (s + 1) * IN_SUB_ROWS)
            acc = jnp.dot(a_ref[rs, :], w_ref[slot], preferred_element_type=F32)
            for half, kind in enumerate(kinds):
                c0 = half * IN_HALF
                epilogues[kind](acc[:, c0:c0 + IN_HALF], rs, c0)

    for kinds in sorted(set(_TILE_KINDS)):
        tiles = [t for t in range(IN_TILES) if _TILE_KINDS[t] == kinds]
        lo_t, hi_t = tiles[0], tiles[-1]
        assert tiles == list(range(lo_t, hi_t + 1))
        pl.when((r - 1 >= lo_t) & (r - 1 <= hi_t))(functools.partial(run, kinds))


def _in_proj(a, w, gains, tables, w_next, seq, tm=1024):
    m, k = a.shape
    steps = m // tm
    pos_blocks = seq // tm
    last = IN_TILES - 1
    rider_chunks = IN_RIDER_ROUNDS * steps
    rider_rows = w_next.shape[0] // rider_chunks
    assert rider_rows * rider_chunks == w_next.shape[0] and rider_rows % 16 == 0
    row = lambda r, i: jnp.where(r == 0, 0, i)
    tile = lambda r, i: jnp.maximum(r - 1, 0)
    tab_spec = pl.BlockSpec((tm, LANES), lambda r, i: (i % pos_blocks, 0))
    rider_spec = pl.BlockSpec((rider_rows, w_next.shape[1]),
                              lambda r, i: (jnp.clip((r - 1) * steps + i, 0, rider_chunks - 1), 0))
    return pl.pallas_call(
        _in_proj_kernel,
        grid=(IN_TILES + 1, steps),
        in_specs=[pl.BlockSpec((tm, k), lambda r, i: (row(r, i), 0)),
                  pl.BlockSpec((k // steps, IN_TN), lambda r, i: (i, jnp.minimum(r, last))),
                  pl.BlockSpec((1, 1, IN_TN), lambda r, i: (tile(r, i), 0, 0)),
                  tab_spec, tab_spec, tab_spec, tab_spec, tab_spec, rider_spec],
        out_specs=[pl.BlockSpec((tm, IN_TN), lambda r, i: (row(r, i), tile(r, i))), rider_spec],
        out_shape=[jax.ShapeDtypeStruct((m, IN_COLS), BF16),
                   jax.ShapeDtypeStruct(w_next.shape, BF16)],
        scratch_shapes=[pltpu.VMEM((2, k, IN_TN), BF16)],
        compiler_params=_params(("arbitrary", "arbitrary")),
        name="in_proj",
    )(a, w, gains, *tables, w_next)


SWA_STEP_BLOCKS = 4


def _swa_kernel(bounded_ref, sink_ref, sinkrow_ref, q_ref, kp_ref, kc_ref, vp_ref, vc_ref, o_ref):
    j = pl.program_id(1)
    nt = (((1,), (1,)), ((), ()))
    rows = SWA_GROUP * BLOCK
    r = lax.broadcasted_iota(jnp.int32, (rows, 2 * BLOCK), 0) & (BLOCK - 1)
    c = lax.broadcasted_iota(jnp.int32, (rows, 2 * BLOCK), 1)
    window = (c > r) & (c <= r + BLOCK)
    valid = [window & ((j > 0) | (c >= BLOCK))] + [window] * (SWA_STEP_BLOCKS - 1)
    ones = jnp.ones((2 * BLOCK, HEAD_DIM), BF16)

    def masked_scores(h, t):
        ks = slice(h * HEAD_DIM, (h + 1) * HEAD_DIM)
        qr = slice(t * BLOCK, (t + 1) * BLOCK)
        if t == 0:
            kk = jnp.concatenate([kp_ref[:, ks], kc_ref[:BLOCK, ks]], axis=0)
            vv = jnp.concatenate([vp_ref[:, ks], vc_ref[:BLOCK, ks]], axis=0)
        else:
            kk = kc_ref[(t - 1) * BLOCK:(t + 1) * BLOCK, ks]
            vv = vc_ref[(t - 1) * BLOCK:(t + 1) * BLOCK, ks]
        heads = [h * SWA_GROUP + g for g in range(SWA_GROUP)]
        q4 = jnp.concatenate([q_ref[qr, hd * HEAD_DIM:(hd + 1) * HEAD_DIM] for hd in heads], axis=0)
        s = lax.dot_general(q4, kk, nt, preferred_element_type=F32)
        return jnp.where(valid[t], s, NEG_INF), vv, heads, qr

    @pl.when(bounded_ref[0] != 0)
    def _():
        shift = bounded_ref[1]
        for h in range(SWA_KV_HEADS):
            for t in range(SWA_STEP_BLOCKS):
                s, vv, heads, qr = masked_scores(h, t)
                vaug = jnp.concatenate([vv, ones], axis=1)
                o = jnp.dot(jnp.exp2(s - shift).astype(BF16), vaug, preferred_element_type=F32)
                for g, hd in enumerate(heads):
                    rs = slice(g * BLOCK, (g + 1) * BLOCK)
                    sink = jnp.exp2(sinkrow_ref[hd:hd + 1, :] * LOG2E - shift)
                    out = o[rs, :HEAD_DIM] / (o[rs, HEAD_DIM:] + sink)
                    o_ref[qr, hd * HEAD_DIM:(hd + 1) * HEAD_DIM] = out.astype(o_ref.dtype)

    @pl.when(bounded_ref[0] == 0)
    def _():
        for h in range(SWA_KV_HEADS):
            for t in range(SWA_STEP_BLOCKS):
                s, vv, heads, qr = masked_scores(h, t)
                for g, hd in enumerate(heads):
                    sg = s[g * BLOCK:(g + 1) * BLOCK]
                    sink = sink_ref[hd] * LOG2E
                    mx = jnp.maximum(jnp.max(sg, axis=-1, keepdims=True), sink)
                    p = jnp.exp2(sg - mx)
                    denom = jnp.sum(p, axis=-1, keepdims=True) + jnp.exp2(sink - mx)
                    o = jnp.dot(p.astype(BF16), vv, preferred_element_type=F32)
                    o_ref[qr, hd * HEAD_DIM:(hd + 1) * HEAD_DIM] = (o / denom).astype(o_ref.dtype)


def _swa_attention(proj, bounded, sinks, batch, seq):
    step_rows = SWA_STEP_BLOCKS * BLOCK
    ns = seq // step_rows
    qw = SWA_HEADS * HEAD_DIM
    kw = SWA_KV_HEADS * HEAD_DIM
    row = lambda b, j: b * ns + j
    prev = lambda b, j: (b * ns + j) * SWA_STEP_BLOCKS - jnp.minimum(j, 1)
    k_blk = SWA_Q_COLS // kw
    v_blk = k_blk + 1
    sink_rows = jnp.broadcast_to(sinks[:, None], (SWA_HEADS, HEAD_DIM))
    return pl.pallas_call(
        _swa_kernel,
        grid=(batch, ns),
        in_specs=[pl.BlockSpec(memory_space=pltpu.SMEM),
                  pl.BlockSpec(memory_space=pltpu.SMEM),
                  pl.BlockSpec((SWA_HEADS, HEAD_DIM), lambda b, j: (0, 0)),
                  pl.BlockSpec((step_rows, qw), lambda b, j: (row(b, j), 0)),
                  pl.BlockSpec((BLOCK, kw), lambda b, j: (prev(b, j), k_blk)),
                  pl.BlockSpec((step_rows, kw), lambda b, j: (row(b, j), k_blk)),
                  pl.BlockSpec((BLOCK, kw), lambda b, j: (prev(b, j), v_blk)),
                  pl.BlockSpec((step_rows, kw), lambda b, j: (row(b, j), v_blk))],
        out_specs=pl.BlockSpec((step_rows, qw), lambda b, j: (row(b, j), 0)),
        out_shape=jax.ShapeDtypeStruct((batch * seq, MIX_HALF), BF16),
        compiler_params=_params(("parallel", "parallel")),
        name="swa_attention",
    )(bounded, sinks, sink_rows, proj, proj, proj, proj, proj)


DIFF_TQ = 2048
DIFF_TK = 256
DIFF_DIAG = DIFF_TQ // DIFF_TK
assert DIFF_DIAG * DIFF_TK == DIFF_TQ and DIFF_DIAG % 2 == 0

def _diff_kernel(bounded_ref, lq1_ref, lk1_ref, lq2_ref, lk2_ref, subg_ref,
                 q1_ref, q2_ref, k1_ref, k2_ref, v_ref, wf_ref, o_ref, wbf_ref,
                 vaug_ref, acc_ref, *, lambda_init):
    _cast_chunk(wf_ref, wbf_ref)
    qi = pl.program_id(2)
    tq = DIFF_TQ
    nt = (((1,), (1,)), ((), ()))
    lane = lax.broadcasted_iota(jnp.int32, (tq, LANES), 1)
    qs = []
    for q_ref in (q1_ref, q2_ref):
        q = q_ref[...]
        for e in range(2):
            keep = (lane >= e * DIFF_QK_DIM) & (lane < (e + 1) * DIFF_QK_DIM)
            qs.append(jnp.where(keep, q, jnp.zeros_like(q)))

    tk = DIFF_TK
    k_refs = (k1_ref, k2_ref)
    row = lax.broadcasted_iota(jnp.int32, (tq, tk), 0)
    col = lax.broadcasted_iota(jnp.int32, (tq, tk), 1)
    causal = col <= row

    def scores(slot, j, r0, masked):
        off = pl.multiple_of(j * tk, tk)
        s = lax.dot_general(qs[slot][r0:], k_refs[slot // 2][pl.ds(off, tk), :], nt,
                            preferred_element_type=F32)
        if masked:
            s = jnp.where(causal[:tq - r0], s, NEG_INF)
        return s

    lam = (jnp.exp(jnp.sum(lq1_ref[...] * lk1_ref[...], axis=-1, keepdims=True))
           - jnp.exp(jnp.sum(lq2_ref[...] * lk2_ref[...], axis=-1, keepdims=True))
           + lambda_init)

    def finalize(e, o1, o2):
        o = o1 - lam * o2
        ms = jnp.mean(o * o, axis=-1, keepdims=True)
        on = o * lax.rsqrt(ms + NORM_EPS) * subg_ref[...]
        o_ref[:, e * HEAD_DIM:(e + 1) * HEAD_DIM] = (on * (1.0 - lambda_init)).astype(o_ref.dtype)

    @pl.when(qi == 0)
    def _():
        for e in range(2):
            vaug_ref[e, :, :HEAD_DIM] = v_ref[:, e * HEAD_DIM:(e + 1) * HEAD_DIM]
            vaug_ref[e, :, HEAD_DIM:] = jnp.ones((v_ref.shape[0], HEAD_DIM), BF16)

    @pl.when(bounded_ref[0] != 0)
    def _():
        shift = bounded_ref[1]

        def step(j, r0, masked, assign=False):
            off = pl.multiple_of(j * tk, tk)
            ps = [jnp.exp2(scores(slot, j, r0, masked) - shift).astype(BF16) for slot in range(4)]
            for e in range(2):
                pp = jnp.concatenate([ps[e], ps[2 + e]], axis=0)
                pv = jnp.dot(pp, vaug_ref[e, pl.ds(off, tk), :], preferred_element_type=F32)
                if assign:
                    acc_ref[e] = pv
                else:
                    acc_ref[e, r0:tq] += pv[:tq - r0]
                    acc_ref[e, tq + r0:] += pv[tq - r0:]

        step(DIFF_DIAG * qi, 0, True, assign=True)
        for d in range(1, DIFF_DIAG):
            step(DIFF_DIAG * qi + d, d * tk, True)

        def body(jj, carry):
            step(2 * jj, 0, False)
            step(2 * jj + 1, 0, False)
            return carry

        lax.fori_loop(0, (DIFF_DIAG * qi) // 2, body, 0)
        for e in range(2):
            a = acc_ref[e]
            finalize(e, a[:tq, :HEAD_DIM] / a[:tq, HEAD_DIM:],
                     a[tq:, :HEAD_DIM] / a[tq:, HEAD_DIM:])

    @pl.when(bounded_ref[0] == 0)
    def _():
        M, L = HEAD_DIM, HEAD_DIM + 1
        acc_ref[:, :, :HEAD_DIM] = jnp.zeros((2, 2 * tq, HEAD_DIM), F32)
        acc_ref[:, :, M:M + 1] = jnp.full((2, 2 * tq, 1), NEG_INF, F32)
        acc_ref[:, :, L:L + 1] = jnp.zeros((2, 2 * tq, 1), F32)

        def step(j, r0, masked):
            off = pl.multiple_of(j * tk, tk)
            vb = v_ref[pl.ds(off, tk), :]
            for slot in range(4):
                mp, e = slot // 2, slot % 2
                rows = slice(mp * tq + r0, (mp + 1) * tq)
                s = scores(slot, j, r0, masked)
                m_prev = acc_ref[e, rows, M:M + 1]
                m_new = jnp.maximum(m_prev, jnp.max(s, axis=-1, keepdims=True))
                alpha = jnp.exp2(m_prev - m_new)
                p = jnp.exp2(s - m_new)
                acc_ref[e, rows, L:L + 1] = (alpha * acc_ref[e, rows, L:L + 1]
                                             + jnp.sum(p, axis=-1, keepdims=True))
                pv = jnp.dot(p.astype(BF16), vb[:, e * HEAD_DIM:(e + 1) * HEAD_DIM],
                             preferred_element_type=F32)
                acc_ref[e, rows, :HEAD_DIM] = alpha * acc_ref[e, rows, :HEAD_DIM] + pv
                acc_ref[e, rows, M:M + 1] = m_new

        def body(j, carry):
            step(j, 0, False)
            return carry

        lax.fori_loop(0, DIFF_DIAG * qi, body, 0)
        for d in range(DIFF_DIAG):
            step(DIFF_DIAG * qi + d, d * tk, True)
        for e in range(2):
            a = acc_ref[e]
            finalize(e, a[:tq, :HEAD_DIM] / a[:tq, L:L + 1], a[tq:, :HEAD_DIM] / a[tq:, L:L + 1])


def _diff_attention(proj, bounded, lq1, lk1, lq2, lk2, subg, w_next, batch, seq, lambda_init):
    tq = DIFF_TQ
    nq = seq // tq
    pairs = DIFF_HEADS // 2
    w_in_spec, w_out_spec, w_out_shape = _cast_rider(w_next, (batch, pairs, nq))
    map_blks = DIFF_HEADS * DIFF_QK_DIM // LANES
    vec = lambda d: pl.BlockSpec((1, d), lambda b, h, i: (0, 0))
    return pl.pallas_call(
        functools.partial(_diff_kernel, lambda_init=lambda_init),
        grid=(batch, pairs, nq),
        in_specs=[pl.BlockSpec(memory_space=pltpu.SMEM),
                  vec(DIFF_QK_DIM), vec(DIFF_QK_DIM), vec(DIFF_QK_DIM), vec(DIFF_QK_DIM),
                  vec(HEAD_DIM),
                  pl.BlockSpec((tq, LANES), lambda b, h, i: (b * nq + i, QB_BLK + h)),
                  pl.BlockSpec((tq, LANES), lambda b, h, i: (b * nq + i, QB_BLK + map_blks + h)),
                  pl.BlockSpec((seq, LANES), lambda b, h, i: (b, KB_BLK + h)),
                  pl.BlockSpec((seq, LANES), lambda b, h, i: (b, KB_BLK + map_blks + h)),
                  pl.BlockSpec((seq, 2 * HEAD_DIM), lambda b, h, i: (b, VB_BLK // 2 + h)),
                  w_in_spec],
        out_specs=[pl.BlockSpec((tq, 2 * HEAD_DIM), lambda b, h, i: (b * nq + i, h)), w_out_spec],
        out_shape=[jax.ShapeDtypeStruct((batch * seq, MIX_HALF), BF16), w_out_shape],
        scratch_shapes=[pltpu.VMEM((2, seq, 2 * HEAD_DIM), BF16),
                        pltpu.VMEM((2, 2 * tq, 2 * HEAD_DIM), F32)],
        compiler_params=_params(("parallel", "parallel", "arbitrary")),
        name="diff_attention",
    )(bounded, lq1.reshape(1, -1), lk1.reshape(1, -1), lq2.reshape(1, -1), lk2.reshape(1, -1),
      subg.reshape(1, -1), proj, proj, proj, proj, proj, w_next)


OUT_SUB_ROWS = 512


def _out_proj_kernel(oa_ref, ob_ref, w_ref, x_ref, g_ref, o_ref, hg_ref, ssq_ref):
    for s in range(o_ref.shape[0] // OUT_SUB_ROWS):
        rs = slice(s * OUT_SUB_ROWS, (s + 1) * OUT_SUB_ROWS)
        acc = jnp.dot(oa_ref[rs, :], w_ref[:MIX_HALF, :], preferred_element_type=F32)
        acc += jnp.dot(ob_ref[rs, :], w_ref[MIX_HALF:, :], preferred_element_type=F32)
        h = x_ref[rs, :] + acc
        o_ref[rs, :] = h
        hg_ref[rs, :] = (h * g_ref[...]).astype(hg_ref.dtype)
        ssq_ref[0, rs, :] = jnp.sum(h * h, axis=-1, keepdims=True)


def _out_proj(oa, ob, w_bf, x2d, g_next, tm=1024, tn=1024):
    m, d = x2d.shape
    tile = pl.BlockSpec((tm, tn), lambda n, i: (i, n))
    return pl.pallas_call(
        _out_proj_kernel,
        grid=(d // tn, m // tm),
        in_specs=[pl.BlockSpec((tm, MIX_HALF), lambda n, i: (i, 0)),
                  pl.BlockSpec((tm, MIX_HALF), lambda n, i: (i, 0)),
                  pl.BlockSpec((2 * MIX_HALF, tn), lambda n, i: (0, n)),
                  tile,
                  pl.BlockSpec((1, tn), lambda n, i: (0, n))],
        out_specs=[tile, tile, pl.BlockSpec((1, tm, 1), lambda n, i: (n, i, 0))],
        out_shape=[jax.ShapeDtypeStruct((m, d), F32),
                   jax.ShapeDtypeStruct((m, d), BF16),
                   jax.ShapeDtypeStruct((d // tn, m, 1), F32)],
        compiler_params=_params(("parallel", "parallel")),
        name="out_proj",
    )(oa, ob, w_bf, x2d, g_next.reshape(1, d))


def _up_kernel(a_ref, ssq_ref, w_ref, wf1_ref, wf2_ref, o_ref, wbf1_ref, wbf2_ref):
    _cast_chunk(wf1_ref, wbf1_ref)
    _cast_chunk(wf2_ref, wbf2_ref)
    k = a_ref.shape[1]
    for s in range(o_ref.shape[0] // OUT_SUB_ROWS):
        rs = slice(s * OUT_SUB_ROWS, (s + 1) * OUT_SUB_ROWS)
        ms = jnp.sum(ssq_ref[:, rs, :], axis=0) * (1.0 / k)
        inv_rms = lax.rsqrt(ms + NORM_EPS)
        acc = jnp.dot(a_ref[rs, :], w_ref[...], preferred_element_type=F32) * inv_rms
        o_ref[rs, :] = jnp.square(jnp.maximum(acc, 0.0)).astype(o_ref.dtype)


def _mlp_up(a, ssq, w_bf, w_next1, w_next2, tm=1024, tn=1024):
    m, k = a.shape
    n = w_bf.shape[1]
    grid = (n // tn, m // tm)
    in1, out1, shape1 = _cast_rider(w_next1, grid)
    in2, out2, shape2 = _cast_rider(w_next2, grid)
    return pl.pallas_call(
        _up_kernel,
        grid=grid,
        in_specs=[pl.BlockSpec((tm, k), lambda j, i: (i, 0)),
                  pl.BlockSpec((ssq.shape[0], tm, 1), lambda j, i: (0, i, 0)),
                  pl.BlockSpec((k, tn), lambda j, i: (0, j)),
                  in1, in2],
        out_specs=[pl.BlockSpec((tm, tn), lambda j, i: (i, j)), out1, out2],
        out_shape=[jax.ShapeDtypeStruct((m, n), BF16), shape1, shape2],
        compiler_params=_params(("parallel", "parallel")),
        name="mlp_up",
    )(a, ssq, w_bf, w_next1, w_next2)


DOWN_SUB_ROWS = 512


def _down_kernel(u_ref, w_ref, h_ref, o_ref, ob_ref):
    kk = pl.program_id(2)

    def accumulate(first):
        for s in range(o_ref.shape[0] // DOWN_SUB_ROWS):
            rs = slice(s * DOWN_SUB_ROWS, (s + 1) * DOWN_SUB_ROWS)
            part = jnp.dot(u_ref[rs, :], w_ref[...], preferred_element_type=F32)
            o_ref[rs, :] = (h_ref[rs, :] if first else o_ref[rs, :]) + part

    pl.when(kk == 0)(lambda: accumulate(True))
    pl.when(kk > 0)(lambda: accumulate(False))

    @pl.when(kk == pl.num_programs(2) - 1)
    def _():
        ob_ref[...] = o_ref[...].astype(ob_ref.dtype)


def _mlp_down(u, w_bf, h2d, tm=1024, tn=1024, tk=4096):
    m, k = u.shape
    d = w_bf.shape[1]
    return pl.pallas_call(
        _down_kernel,
        grid=(d // tn, m // tm, k // tk),
        in_specs=[pl.BlockSpec((tm, tk), lambda j, i, kk: (i, kk)),
                  pl.BlockSpec((tk, tn), lambda j, i, kk: (kk, j)),
                  pl.BlockSpec((tm, tn), lambda j, i, kk: (i, j))],
        out_specs=[pl.BlockSpec((tm, tn), lambda j, i, kk: (i, j)),
                   pl.BlockSpec((tm, tn), lambda j, i, kk: (i, j))],
        out_shape=[jax.ShapeDtypeStruct((m, d), F32),
                   jax.ShapeDtypeStruct((m, d), BF16)],
        compiler_params=_params(("parallel", "parallel", "arbitrary")),
        name="mlp_down",
    )(u, w_bf, h2d)


def _ple_kernel(p_ref, w_ref, g_ref, o_ref):
    y = jnp.dot(p_ref[...].astype(BF16), w_ref[...], preferred_element_type=F32)
    ms = jnp.mean(y * y, axis=-1, keepdims=True)
    o_ref[...] = y * lax.rsqrt(ms + NORM_EPS) * g_ref[...]


def _ple_embed(p2d, w_bf, g, tm=512):
    m, k = p2d.shape
    d = w_bf.shape[1]
    return pl.pallas_call(
        _ple_kernel,
        grid=(m // tm,),
        in_specs=[pl.BlockSpec((tm, k), lambda i: (i, 0)),
                  pl.BlockSpec((k, d), lambda i: (0, 0)),
                  pl.BlockSpec((1, d), lambda i: (0, 0))],
        out_specs=pl.BlockSpec((tm, d), lambda i: (i, 0)),
        out_shape=jax.ShapeDtypeStruct((m, d), F32),
        compiler_params=_params(("parallel",)),
        name="ple_embed",
    )(p2d, w_bf, g.reshape(1, d))


def _gate_kernel(hb_ref, w_ref, h_ref, pe_ref, o_ref):
    for s in range(o_ref.shape[0] // OUT_SUB_ROWS):
        rs = slice(s * OUT_SUB_ROWS, (s + 1) * OUT_SUB_ROWS)
        z = jnp.dot(hb_ref[rs, :], w_ref[...], preferred_element_type=F32)
        o_ref[rs, :] = h_ref[rs, :] + jax.nn.sigmoid(z) * pe_ref[rs, :]


def _ple_gate(hb, w_bf, h2d, pe, tm=512, tn=1024):
    m, d = h2d.shape
    tile = pl.BlockSpec((tm, tn), lambda j, i: (i, j))
    return pl.pallas_call(
        _gate_kernel,
        grid=(d // tn, m // tm),
        in_specs=[pl.BlockSpec((tm, d), lambda j, i: (i, 0)),
                  pl.BlockSpec((d, tn), lambda j, i: (0, j)),
                  tile, tile],
        out_specs=tile,
        out_shape=jax.ShapeDtypeStruct((m, d), F32),
        compiler_params=_params(("parallel", "parallel")),
        name="ple_gate",
    )(hb, w_bf, h2d, pe)


def _rope_tables(seq):
    pos = jnp.arange(seq, dtype=F32)

    def tab(dim):
        inv = 1.0 / (ROPE_THETA ** (jnp.arange(0, dim, 2, dtype=F32) / dim))
        ang = pos[:, None] * inv[None, :]
        ang = jnp.concatenate([ang, ang], axis=-1)
        return jnp.cos(ang), jnp.sin(ang)

    lane = jnp.arange(LANES)
    cos_a, sin_a = tab(HEAD_DIM)
    sin_a = jnp.where(lane < HEAD_DIM // 2, -sin_a, sin_a)
    cos_b, sin_b = tab(DIFF_QK_DIM)
    cos_b = jnp.concatenate([cos_b, cos_b], axis=-1)
    sin_b = jnp.concatenate([sin_b, sin_b], axis=-1)
    first_half = (lane % DIFF_QK_DIM) < DIFF_QK_DIM // 2
    sin_b_lo = jnp.where(first_half, -sin_b, 0.0)
    sin_b_hi = jnp.where(first_half, 0.0, sin_b)
    return cos_a, sin_a, cos_b, sin_b_lo, sin_b_hi


def _gain_rows(swa_q_g, swa_k_g, diff_q_g, diff_k_g):
    qa = jnp.tile(swa_q_g, SWA_Q_COLS // HEAD_DIM) * (LOG2E / math.sqrt(HEAD_DIM))
    ka = jnp.tile(swa_k_g, SWA_KV_COLS // HEAD_DIM)
    qb = jnp.tile(diff_q_g, DIFF_QK_COLS // DIFF_QK_DIM) * (LOG2E / math.sqrt(DIFF_QK_DIM))
    kb = jnp.tile(diff_k_g, DIFF_QK_COLS // DIFF_QK_DIM)
    cols = jnp.concatenate([qa, ka, jnp.ones((SWA_KV_COLS,), F32), qb, kb,
                            jnp.ones((DIFF_V_COLS,), F32)])
    return cols.reshape(IN_TILES, 1, IN_TN)


def _scores_bounded(q_g, k_g, dim, sinks=None):
    rounding_slack = 1.02
    bound = (math.sqrt(dim) * LOG2E * rounding_slack
             * jnp.max(jnp.abs(q_g)) * jnp.max(jnp.abs(k_g)))
    if sinks is not None:
        bound = jnp.maximum(bound, LOG2E * jnp.max(jnp.abs(sinks)))
    return jnp.stack([(bound <= SCORE_BOUND).astype(F32), bound.astype(F32)])


def kernel(x, p, attn_norm_g, w_in, swa_q_norm_g, swa_k_norm_g, swa_sinks, diff_q_norm_g, diff_k_norm_g, diff_lambda_q1, diff_lambda_k1, diff_lambda_q2, diff_lambda_k2, diff_subln_g, w_o, mlp_norm_g, w_up, w_down, w_ple_proj, ple_norm_g, w_ple_gate):
    batch, seq, d = x.shape
    depth = w_in.shape[0]
    tables = _rope_tables(seq)
    h = x.reshape(batch * seq, d)
    for i in range(depth):
        lambda_init = 0.8 - 0.6 * math.exp(-0.3 * i)
        a = _rmsnorm_rows(h, attn_norm_g[i])
        gains = _gain_rows(swa_q_norm_g[i], swa_k_norm_g[i], diff_q_norm_g[i], diff_k_norm_g[i])
        proj, w_o_bf = _in_proj(a, w_in[i], gains, tables, w_o[i], seq)
        bounded_a = _scores_bounded(swa_q_norm_g[i], swa_k_norm_g[i], HEAD_DIM, swa_sinks[i])
        out_a = _swa_attention(proj, bounded_a, swa_sinks[i], batch, seq)
        bounded_b = _scores_bounded(diff_q_norm_g[i], diff_k_norm_g[i], DIFF_QK_DIM)
        out_b, w_up_bf = _diff_attention(proj, bounded_b, diff_lambda_q1[i], diff_lambda_k1[i],
                                         diff_lambda_q2[i], diff_lambda_k2[i], diff_subln_g[i],
                                         w_up[i], batch, seq, lambda_init)
        h, hg, ssq = _out_proj(out_a, out_b, w_o_bf, h, mlp_norm_g[i])
        u, w_down_bf, w_gate_bf = _mlp_up(hg, ssq, w_up_bf, w_down[i], w_ple_gate[i])
        h, hb = _mlp_down(u, w_down_bf, h)
        pe = _ple_embed(p[i].reshape(batch * seq, PLE_DIM), w_ple_proj[i].astype(BF16), ple_norm_g[i])
        h = _ple_gate(hb, w_gate_bf, h, pe)
    return h.reshape(batch, seq, d)
```
